```python
import math
import jax, jax.numpy as jnp
from jax import lax
import numpy as np

D_MODEL = 4096
BATCH = 4
SEQ = 2048
DEPTH = 2

CHUNK = 64
PLE_DIM = 256
N_A_LAYERS = DEPTH // 2
N_B_LAYERS = DEPTH - N_A_LAYERS
POOL_WINDOWS = (2, 4, 8, 16)
N_POOL_GROUPS = len(POOL_WINDOWS)
POOL_GROUP_DIM = D_MODEL // N_POOL_GROUPS
SB_HEAD_DIM = 128
SB_HEADS = D_MODEL // SB_HEAD_DIM
Q_BLOCK = 128
N_EXPERTS = 64
TOP_K = 8
D_EXPERT = (D_MODEL * 3) // 32
D_SHARED = D_EXPERT
ROUTED_SCALE = 2.5
EXPERT_BLOCK = 16
LN_EPS = 1e-5
N_SUBLAYERS = 3
DEEPNORM_ALPHA = (2.0 * DEPTH) ** 0.25
DEEPNORM_BETA = (8.0 * DEPTH) ** -0.25

kernel_name = 'yoco_pool_stickbreak_moe_deepnorm'


def layer_norm(x, g, b):
    xf = x.astype(jnp.float32)
    mu = jnp.mean(xf, axis=-1, keepdims=True)
    var = jnp.mean(jnp.square(xf - mu), axis=-1, keepdims=True)
    return ((xf - mu) * lax.rsqrt(var + LN_EPS) * g.astype(jnp.float32) + b.astype(jnp.float32)).astype(x.dtype)


def post_norm(x, y, g, b):
    return layer_norm(DEEPNORM_ALPHA * x + y, g, b)


def pool_mixer(x, w_in, w_grp, scale, w_out):
    bsz, seq, _ = x.shape
    u = (x @ w_in).reshape(bsz, seq, N_POOL_GROUPS, POOL_GROUP_DIM).astype(jnp.float32)
    t = jnp.arange(1, seq + 1, dtype=jnp.float32)
    outs = []
    for g, w in enumerate(POOL_WINDOWS):
        ug = u[:, :, g]
        cs = jnp.cumsum(ug, axis=1)
        lag = jnp.pad(cs, ((0, 0), (w, 0), (0, 0)))[:, :seq]
        cnt = jnp.minimum(t, float(w))[None, :, None]
        outs.append((cs - lag) / cnt - ug)
    pooled = jnp.stack(outs, axis=2).astype(x.dtype)
    mixed = jnp.einsum('bsgc,gcd->bsgd', pooled, w_grp).reshape(bsz, seq, D_MODEL) * scale
    return mixed @ w_out


def split_heads(y):
    bsz, seq, _ = y.shape
    return y.reshape(bsz, seq, SB_HEADS, SB_HEAD_DIM).transpose(0, 2, 1, 3)


def stick_breaking_attention(x, w_q, k, v, w_o):
    bsz, seq, _ = x.shape
    q = split_heads(x @ w_q)
    inv_sqrt_d = 1.0 / math.sqrt(SB_HEAD_DIM)
    outs = []
    for blk in range(seq // Q_BLOCK):
        start = blk * Q_BLOCK
        end = start + Q_BLOCK
        qb = q[:, :, start:end]
        kb = k[:, :, :end]
        vb = v[:, :, :end]
        z = jnp.einsum('bhqd,bhkd->bhqk', qb, kb).astype(jnp.float32) * inv_sqrt_d
        mask = jnp.arange(end)[None, :] < jnp.arange(start, end)[:, None]
        log_fail = jnp.where(mask, jax.nn.log_sigmoid(-z), 0.0)
        between = lax.cumsum(log_fail, axis=3, reverse=True) - log_fail
        a = jnp.where(mask, jnp.exp(jax.nn.log_sigmoid(z) + between), 0.0)
        outs.append(jnp.einsum('bhqk,bhkd->bhqd', a.astype(vb.dtype), vb))
    o = jnp.concatenate(outs, axis=2).transpose(0, 2, 1, 3).reshape(bsz, seq, D_MODEL)
    return o @ w_o


def swiglu(x, w_gate, w_up, w_down):
    return (jax.nn.silu(x @ w_gate) * (x @ w_up)) @ w_down


def moe(x, w_router, bias, w_gate, w_up, w_down, ws_gate, ws_up, ws_down):
    bsz, seq, _ = x.shape
    xt = x.reshape(-1, D_MODEL)
    scores = jax.nn.sigmoid((xt @ w_router).astype(jnp.float32))
    _, idx = lax.top_k(scores + bias.astype(jnp.float32), TOP_K)
    sel = jnp.take_along_axis(scores, idx, axis=-1)
    wts = sel / jnp.sum(sel, axis=-1, keepdims=True) * ROUTED_SCALE
    gates = jnp.sum(jax.nn.one_hot(idx, N_EXPERTS, dtype=jnp.float32) * wts[..., None], axis=1).astype(x.dtype)
    y = swiglu(xt, ws_gate, ws_up, ws_down)
    for e0 in range(0, N_EXPERTS, EXPERT_BLOCK):
        sl = slice(e0, e0 + EXPERT_BLOCK)
        h = jax.nn.silu(jnp.einsum('td,edf->tef', xt, w_gate[sl])) * jnp.einsum('td,edf->tef', xt, w_up[sl])
        y = y + jnp.einsum('tef,efd->td', h * gates[:, sl, None], w_down[sl])
    return y.reshape(bsz, seq, D_MODEL)


def per_layer_embedding(x, p_i, w_proj, w_gate, b_gate):
    return jax.nn.sigmoid(x @ w_gate + b_gate) * (p_i @ w_proj)


def setup_inputs(seed: int = 0) -> dict:
    key = jax.random.key(seed)
    ks = jax.random.split(key, 24)
    f32 = jnp.float32
    D, G, Dg, E, F = D_MODEL, N_POOL_GROUPS, POOL_GROUP_DIM, N_EXPERTS, D_EXPERT
    nrm = lambda k, shape, s: jax.random.normal(k, shape, f32) * s
    return {
        'x': nrm(ks[0], (BATCH, SEQ, D), 1.0),
        'p': nrm(ks[1], (DEPTH, BATCH, SEQ, PLE_DIM), 1.0),
        'pool_w_in': nrm(ks[2], (N_A_LAYERS, D, D), D ** -0.5),
        'pool_w_grp': nrm(ks[3], (N_A_LAYERS, G, Dg, Dg), Dg ** -0.5),
        'pool_scale': 1.0 + nrm(ks[4], (N_A_LAYERS, D), 0.1),
        'pool_w_out': nrm(ks[5], (N_A_LAYERS, D, D), D ** -0.5 * DEEPNORM_BETA),
        'kv_w_k': nrm(ks[6], (D, D), D ** -0.5),
        'kv_w_v': nrm(ks[7], (D, D), D ** -0.5 * DEEPNORM_BETA),
        'sb_w_q': nrm(ks[8], (N_B_LAYERS, D, D), D ** -0.5),
        'sb_w_o': nrm(ks[9], (N_B_LAYERS, D, D), D ** -0.5 * DEEPNORM_BETA),
        'moe_w_router': nrm(ks[10], (DEPTH, D, E), D ** -0.5),
        'moe_bias': nrm(ks[11], (DEPTH, E), 0.01),
        'moe_w_gate': nrm(ks[12], (DEPTH, E, D, F), D ** -0.5),
        'moe_w_up': nrm(ks[13], (DEPTH, E, D, F), D ** -0.5),
        'moe_w_down': nrm(ks[14], (DEPTH, E, F, D), F ** -0.5 * DEEPNORM_BETA),
        'shared_w_gate': nrm(ks[15], (DEPTH, D, D_SHARED), D ** -0.5),
        'shared_w_up': nrm(ks[16], (DEPTH, D, D_SHARED), D ** -0.5),
        'shared_w_down': nrm(ks[17], (DEPTH, D_SHARED, D), D_SHARED ** -0.5 * DEEPNORM_BETA),
        'ple_w_proj': nrm(ks[18], (DEPTH, PLE_DIM, D), PLE_DIM ** -0.5 * DEEPNORM_BETA),
        'ple_w_gate': nrm(ks[19], (DEPTH, D, D), D ** -0.5),
        'ple_b_gate': nrm(ks[20], (DEPTH, D), 0.01),
        'ln_g': 1.0 + nrm(ks[21], (DEPTH, N_SUBLAYERS, D), 0.01),
        'ln_b': nrm(ks[22], (DEPTH, N_SUBLAYERS, D), 0.01),
    }


def reference(x, p, pool_w_in, pool_w_grp, pool_scale, pool_w_out, kv_w_k, kv_w_v,
              sb_w_q, sb_w_o, moe_w_router, moe_bias, moe_w_gate, moe_w_up, moe_w_down,
              shared_w_gate, shared_w_up, shared_w_down, ple_w_proj, ple_w_gate, ple_b_gate,
              ln_g, ln_b):
    h = x
    k_shared = None
    v_shared = None
    for i in range(DEPTH):
        if i < N_A_LAYERS:
            mix = pool_mixer(h, pool_w_in[i], pool_w_grp[i], pool_scale[i], pool_w_out[i])
        else:
            if i == N_A_LAYERS:
                k_shared = split_heads(h @ kv_w_k)
                v_shared = split_heads(h @ kv_w_v)
            j = i - N_A_LAYERS
            mix = stick_breaking_attention(h, sb_w_q[j], k_shared, v_shared, sb_w_o[j])
        h = post_norm(h, mix, ln_g[i, 0], ln_b[i, 0])
        ffn = moe(h, moe_w_router[i], moe_bias[i], moe_w_gate[i], moe_w_up[i], moe_w_down[i],
                  shared_w_gate[i], shared_w_up[i], shared_w_down[i])
        h = post_norm(h, ffn, ln_g[i, 1], ln_b[i, 1])
        ple = per_layer_embedding(h, p[i], ple_w_proj[i], ple_w_gate[i], ple_b_gate[i])
        h = post_norm(h, ple, ln_g[i, 2], ln_b[i, 2])
    return h
```

```python
import functools
import math

import jax
import jax.numpy as jnp
from jax import lax
from jax.experimental import pallas as pl
from jax.experimental.pallas import tpu as pltpu

BF16, F32, I32 = jnp.bfloat16, jnp.float32, jnp.int32

DEPTH = 2
POOL_WINDOWS = (2, 4, 8, 16)
POOL_HALO = 16
HEAD_DIM = 128
TOP_K = 8
ROUTED_SCALE = 2.5
LN_EPS = 1e-5
DEEPNORM_ALPHA = (2.0 * DEPTH) ** 0.25
EXP_ZERO_BELOW = -110.0
VMEM_LIMIT_BYTES = 56 * 1024 * 1024
EXPERT_TILE_ROWS = 128


def _params(n_axes=1):
    return pltpu.CompilerParams(dimension_semantics=("arbitrary",) * n_axes,
                                vmem_limit_bytes=VMEM_LIMIT_BYTES)


def _dot(a, b):
    return jnp.dot(a, b, preferred_element_type=F32)


def _sigmoid(x):
    return 1.0 / (1.0 + jnp.exp(-x))


def _mm_kernel(x_ref, w_ref, o_ref):
    o_ref[...] = _dot(x_ref[...], w_ref[...]).astype(o_ref.dtype)


def _matmul(x, w, *, tm=1024, tn=1024, out_dtype=BF16):
    m, k = x.shape
    n = w.shape[1]
    tm, tn = min(tm, m), min(tn, n)
    return pl.pallas_call(
        _mm_kernel,
        grid=(m // tm, n // tn),
        in_specs=[pl.BlockSpec((tm, k), lambda i, j: (i, 0)),
                  pl.BlockSpec((k, tn), lambda i, j: (0, j))],
        out_specs=pl.BlockSpec((tm, tn), lambda i, j: (i, j)),
        out_shape=jax.ShapeDtypeStruct((m, n), out_dtype),
        compiler_params=_params(2),
        name="matmul",
    )(x, w)


def _pool_grp_kernel(u_ref, halo_ref, wg_ref, sc_ref, o_ref, *, tm, tiles_per_seq):
    g = pl.program_id(0)
    seq_tile = pl.program_id(1) % tiles_per_seq
    cur = u_ref[...].astype(F32)
    halo = halo_ref[...].astype(F32)
    halo = jnp.where(seq_tile == 0, 0.0, halo)
    ext = jnp.concatenate([halo, cur], axis=0)
    row = lax.broadcasted_iota(I32, (tm, 1), 0)
    pos1 = (seq_tile * tm + row + 1).astype(F32)

    for gi, w in enumerate(POOL_WINDOWS):
        @pl.when(g == gi)
        def _(w=w):
            s = ext
            span = 1
            while span < w:
                s = s + pltpu.roll(s, span, 0)
                span *= 2
            win = s[POOL_HALO:]
            cnt = jnp.minimum(pos1, float(w))
            pooled = win / cnt - cur
            mixed = _dot(pooled.astype(BF16), wg_ref[...]) * sc_ref[...]
            o_ref[...] = mixed.astype(o_ref.dtype)


def _pool_grp(u, w_grp, scale, seq, *, tm=512):
    t, d = u.shape
    ng, dg, _ = w_grp.shape
    tm = min(tm, seq)
    hb = tm // POOL_HALO
    return pl.pallas_call(
        functools.partial(_pool_grp_kernel, tm=tm, tiles_per_seq=seq // tm),
        grid=(ng, t // tm),
        in_specs=[pl.BlockSpec((tm, dg), lambda g, i: (i, g)),
                  pl.BlockSpec((POOL_HALO, dg), lambda g, i: (jnp.maximum(i * hb - 1, 0), g)),
                  pl.BlockSpec((None, dg, dg), lambda g, i: (g, 0, 0)),
                  pl.BlockSpec((1, dg), lambda g, i: (0, g))],
        out_specs=pl.BlockSpec((tm, dg), lambda g, i: (i, g)),
        out_shape=jax.ShapeDtypeStruct((t, d), BF16),
        compiler_params=_params(2),
        name="pool_grp",
    )(u, u, w_grp, scale.reshape(1, d))


def _row_stats_merge(j, tn, z, mean_ref, m2_ref):
    cm = jnp.mean(z, axis=-1, keepdims=True)
    dz = z - cm
    cm2 = jnp.sum(dz * dz, axis=-1, keepdims=True)

    @pl.when(j == 0)
    def _():
        mean_ref[...] = cm
        m2_ref[...] = cm2

    @pl.when(j > 0)
    def _():
        n_a = (j * tn).astype(F32)
        tot = n_a + float(tn)
        delta = cm - mean_ref[...]
        mean_ref[...] = mean_ref[...] + delta * (float(tn) / tot)
        m2_ref[...] = m2_ref[...] + cm2 + delta * delta * (n_a * float(tn) / tot)


def _mm_ln_kernel(*refs, nj, tn, d, gated, n_out):
    if gated:
        lhs_ref, w_ref, res_ref, g_ref, b_ref, p_ref, wp_ref, bg_ref = refs[:8]
        rest = refs[8:]
    else:
        lhs_ref, w_ref, res_ref, g_ref, b_ref = refs[:5]
        rest = refs[5:]
    outs, (acc_ref, mean_ref, m2_ref) = rest[:n_out], rest[n_out:]
    j = pl.program_id(1)

    y = _dot(lhs_ref[...], w_ref[...])
    if gated:
        y = _sigmoid(y + bg_ref[...]) * _dot(p_ref[...], wp_ref[...])
    z = DEEPNORM_ALPHA * res_ref[...].astype(F32) + y
    acc_ref[j] = z
    _row_stats_merge(j, tn, z, mean_ref, m2_ref)

    @pl.when(j == nj - 1)
    def _():
        mu = mean_ref[...]
        rstd = lax.rsqrt(m2_ref[...] * (1.0 / d) + LN_EPS)
        for jj in range(nj):
            sl = slice(jj * tn, (jj + 1) * tn)
            hn = (acc_ref[jj] - mu) * rstd * g_ref[:, sl] + b_ref[:, sl]
            for o_ref in outs:
                o_ref[:, sl] = hn.astype(o_ref.dtype)


def _mm_ln(lhs, w, res, ln_g, ln_b, *, out_dtypes=(BF16,), gate=None, tm=512, tn=512):
    t, k = lhs.shape
    d = w.shape[1]
    tm, tn = min(tm, t), min(tn, d)
    nj = d // tn
    in_specs = [pl.BlockSpec((tm, k), lambda i, j: (i, 0)),
                pl.BlockSpec((k, tn), lambda i, j: (0, j)),
                pl.BlockSpec((tm, tn), lambda i, j: (i, j)),
                pl.BlockSpec((1, d), lambda i, j: (0, 0)),
                pl.BlockSpec((1, d), lambda i, j: (0, 0))]
    args = [lhs, w, res, ln_g.reshape(1, d), ln_b.reshape(1, d)]
    if gate is not None:
        p, w_proj, b_gate = gate
        pd = p.shape[1]
        in_specs += [pl.BlockSpec((tm, pd), lambda i, j: (i, 0)),
                     pl.BlockSpec((pd, tn), lambda i, j: (0, j)),
                     pl.BlockSpec((1, tn), lambda i, j: (0, j))]
        args += [p, w_proj, b_gate.reshape(1, d)]
    outs = pl.pallas_call(
        functools.partial(_mm_ln_kernel, nj=nj, tn=tn, d=d, gated=gate is not None,
                          n_out=len(out_dtypes)),
        grid=(t // tm, nj),
        in_specs=in_specs,
        out_specs=[pl.BlockSpec((tm, d), lambda i, j: (i, 0)) for _ in out_dtypes],
        out_shape=[jax.ShapeDtypeStruct((t, d), dt) for dt in out_dtypes],
        scratch_shapes=[pltpu.VMEM((nj, tm, tn), F32),
                        pltpu.VMEM((tm, 1), F32),
                        pltpu.VMEM((tm, 1), F32)],
        compiler_params=_params(2),
        name="mm_ln_gated" if gate is not None else "mm_ln",
    )(*args)
    return outs


def _router_kernel(h_ref, whi_ref, wlo_ref, bias_ref, idx_ref, wts_ref, rank_ref, cnt_ref,
                   carry_ref, *, tm, n_exp):
    @pl.when(pl.program_id(0) == 0)
    def _():
        carry_ref[...] = jnp.zeros_like(carry_ref)

    h = h_ref[...]
    scores = _sigmoid(_dot(h, whi_ref[...]) + _dot(h, wlo_ref[...]))
    sel = scores + bias_ref[...]
    lane = lax.broadcasted_iota(I32, (tm, n_exp), 1)
    lane_k = lax.broadcasted_iota(I32, (tm, TOP_K), 1)
    mask = jnp.zeros((tm, n_exp), F32)
    idxs = jnp.zeros((tm, TOP_K), I32)
    wsel = jnp.zeros((tm, TOP_K), F32)
    for k in range(TOP_K):
        m = jnp.max(sel, axis=-1, keepdims=True)
        ik = jnp.min(jnp.where(sel == m, lane, n_exp), axis=-1, keepdims=True)
        onehot = lane == ik
        mask = jnp.where(onehot, 1.0, mask)
        sel = jnp.where(onehot, -jnp.inf, sel)
        sk = jnp.sum(jnp.where(onehot, scores, 0.0), axis=-1, keepdims=True)
        idxs = jnp.where(lane_k == k, ik, idxs)
        wsel = jnp.where(lane_k == k, sk, wsel)
    idx_ref[...] = idxs
    wts_ref[...] = wsel / jnp.sum(wsel, axis=-1, keepdims=True) * ROUTED_SCALE

    r = lax.broadcasted_iota(I32, (tm, tm), 0)
    c = lax.broadcasted_iota(I32, (tm, tm), 1)
    lower = jnp.where(c < r, 1.0, 0.0).astype(BF16)
    rank_ref[...] = carry_ref[...] + _dot(lower, mask.astype(BF16))
    carry_ref[...] = carry_ref[...] + jnp.sum(mask, axis=0, keepdims=True)
    cnt_ref[...] = carry_ref[...]


def _router(h, w_hi, w_lo, bias, *, tm=512):
    t, d = h.shape
    n_exp = w_hi.shape[1]
    tm = min(tm, t)
    return pl.pallas_call(
        functools.partial(_router_kernel, tm=tm, n_exp=n_exp),
        grid=(t // tm,),
        in_specs=[pl.BlockSpec((tm, d), lambda i: (i, 0)),
                  pl.BlockSpec((d, n_exp), lambda i: (0, 0)),
                  pl.BlockSpec((d, n_exp), lambda i: (0, 0)),
                  pl.BlockSpec((1, n_exp), lambda i: (0, 0))],
        out_specs=[pl.BlockSpec((tm, TOP_K), lambda i: (i, 0)),
                   pl.BlockSpec((tm, TOP_K), lambda i: (i, 0)),
                   pl.BlockSpec((tm, n_exp), lambda i: (i, 0)),
                   pl.BlockSpec((1, n_exp), lambda i: (0, 0))],
        out_shape=[jax.ShapeDtypeStruct((t, TOP_K), I32),
                   jax.ShapeDtypeStruct((t, TOP_K), F32),
                   jax.ShapeDtypeStruct((t, n_exp), F32),
                   jax.ShapeDtypeStruct((1, n_exp), F32)],
        scratch_shapes=[pltpu.VMEM((1, n_exp), F32)],
        compiler_params=_params(1),
        name="router",
    )(h, w_hi, w_lo, bias.reshape(1, n_exp))


def _pos_kernel(idx_ref, rank_ref, off_ref, pos_ref, *, tm, n_exp):
    base = rank_ref[...] + off_ref[...]
    idx = idx_ref[...]
    lane = lax.broadcasted_iota(I32, (tm, n_exp), 1)
    lane_k = lax.broadcasted_iota(I32, (tm, TOP_K), 1)
    pos = jnp.zeros((tm, TOP_K), F32)
    for k in range(TOP_K):
        pk = jnp.sum(jnp.where(lane == idx[:, k:k + 1], base, 0.0), axis=-1, keepdims=True)
        pos = jnp.where(lane_k == k, pk, pos)
    pos_ref[...] = pos.astype(I32)


def _positions(idx, rank, row_off, *, tm=512):
    t, n_exp = rank.shape
    tm = min(tm, t)
    return pl.pallas_call(
        functools.partial(_pos_kernel, tm=tm, n_exp=n_exp),
        grid=(t // tm,),
        in_specs=[pl.BlockSpec((tm, TOP_K), lambda i: (i, 0)),
                  pl.BlockSpec((tm, n_exp), lambda i: (i, 0)),
                  pl.BlockSpec((1, n_exp), lambda i: (0, 0))],
        out_specs=pl.BlockSpec((tm, TOP_K), lambda i: (i, 0)),
        out_shape=jax.ShapeDtypeStruct((t, TOP_K), I32),
        compiler_params=_params(1),
        name="positions",
    )(idx, rank, row_off)


def _dispatch_kernel(pad_lo_ref, pad_hi_ref, pos_ref, h_ref, zero_ref, xs_ref, sem, *, td, n_exp):
    i = pl.program_id(0)

    def row_copy(src_ref, src_row, dst_row):
        return pltpu.make_async_copy(src_ref.at[pl.ds(src_row, 1)], xs_ref.at[pl.ds(dst_row, 1)], sem)

    def issue(n, carry):
        for k in range(TOP_K):
            row_copy(h_ref, i * td + n, pos_ref[n * TOP_K + k]).start()
        return carry

    def drain(n, carry):
        for k in range(TOP_K):
            row_copy(h_ref, 0, 0).wait()
        return carry

    lax.fori_loop(0, td, issue, 0)
    lax.fori_loop(0, td, drain, 0)

    @pl.when(i == 0)
    def _():
        def per_expert(e, carry):
            lo, hi = pad_lo_ref[e], pad_hi_ref[e]

            def zissue(r, c):
                row_copy(zero_ref, 0, r).start()
                return c

            def zdrain(r, c):
                row_copy(zero_ref, 0, 0).wait()
                return c

            lax.fori_loop(lo, hi, zissue, 0)
            lax.fori_loop(lo, hi, zdrain, 0)
            return carry

        lax.fori_loop(0, n_exp, per_expert, 0)


def _dispatch(pos_flat, h, pad_lo, pad_hi, n_rows, *, td=256):
    t, d = h.shape
    td = min(td, t)
    n_exp = pad_lo.shape[0]
    grid_spec = pltpu.PrefetchScalarGridSpec(
        num_scalar_prefetch=2,
        grid=(t // td,),
        in_specs=[pl.BlockSpec((td * TOP_K,), lambda i, lo, hi: (i,), memory_space=pltpu.SMEM),
                  pl.BlockSpec(memory_space=pl.ANY),
                  pl.BlockSpec(memory_space=pl.ANY)],
        out_specs=pl.BlockSpec(memory_space=pl.ANY),
        scratch_shapes=[pltpu.SemaphoreType.DMA(())],
    )
    return pl.pallas_call(
        functools.partial(_dispatch_kernel, td=td, n_exp=n_exp),
        grid_spec=grid_spec,
        out_shape=jax.ShapeDtypeStruct((n_rows, d), h.dtype),
        compiler_params=pltpu.CompilerParams(dimension_semantics=("arbitrary",),
                                             vmem_limit_bytes=VMEM_LIMIT_BYTES,
                                             has_side_effects=True),
        name="dispatch",
    )(pad_lo, pad_hi, pos_flat, h, jnp.zeros((8, d), h.dtype))


def _expert_kernel(te_ref, ts_ref, first_ref, nxt_ref, nused_ref,
                   xs_ref, wg_hbm, wu_hbm, wd_hbm, ye_ref,
                   wg_f, wu_f, wd_f, wg_b, wu_b, wd_b, sem):
    j = pl.program_id(0)

    def weight_copies(e):
        return (pltpu.make_async_copy(wg_hbm.at[e], wg_f, sem.at[0]),
                pltpu.make_async_copy(wu_hbm.at[e], wu_f, sem.at[1]),
                pltpu.make_async_copy(wd_hbm.at[e], wd_f, sem.at[2]))

    @pl.when(j == 0)
    def _():
        for cp in weight_copies(te_ref[0]):
            cp.start()

    @pl.when(first_ref[j] == 1)
    def _():
        for cp in weight_copies(te_ref[j]):
            cp.wait()
        wg_b[...] = wg_f[...].astype(BF16)
        wu_b[...] = wu_f[...].astype(BF16)
        wd_b[...] = wd_f[...].astype(BF16)

        @pl.when(nxt_ref[j] >= 0)
        def _():
            for cp in weight_copies(nxt_ref[j]):
                cp.start()

    @pl.when(j < nused_ref[0])
    def _():
        x = xs_ref[...].astype(BF16)
        g = _dot(x, wg_b[...])
        u = _dot(x, wu_b[...])
        hm = (g * _sigmoid(g) * u).astype(BF16)
        ye_ref[...] = _dot(hm, wd_b[...]).astype(ye_ref.dtype)


def _experts(tile_expert, tile_src, tile_first, tile_next, n_used, xs, w_gate, w_up, w_down, *, tm):
    n_rows, d = xs.shape
    f = w_gate.shape[2]
    grid_spec = pltpu.PrefetchScalarGridSpec(
        num_scalar_prefetch=5,
        grid=(n_rows // tm,),
        in_specs=[pl.BlockSpec((tm, d), lambda j, te, ts, fi, nx, nu: (ts[j], 0)),
                  pl.BlockSpec(memory_space=pl.ANY),
                  pl.BlockSpec(memory_space=pl.ANY),
                  pl.BlockSpec(memory_space=pl.ANY)],
        out_specs=pl.BlockSpec((tm, d), lambda j, te, ts, fi, nx, nu: (ts[j], 0)),
        scratch_shapes=[pltpu.VMEM((d, f), F32), pltpu.VMEM((d, f), F32), pltpu.VMEM((f, d), F32),
                        pltpu.VMEM((d, f), BF16), pltpu.VMEM((d, f), BF16), pltpu.VMEM((f, d), BF16),
                        pltpu.SemaphoreType.DMA((3,))],
    )
    return pl.pallas_call(
        _expert_kernel,
        grid_spec=grid_spec,
        out_shape=jax.ShapeDtypeStruct((n_rows, d), xs.dtype),
        compiler_params=_params(1),
        name="experts",
    )(tile_expert, tile_src, tile_first, tile_next, n_used, xs, w_gate, w_up, w_down)


def _shared_kernel(h_ref, wg_ref, wu_ref, wd_ref, o_ref):
    h = h_ref[...]
    g = _dot(h, wg_ref[...])
    u = _dot(h, wu_ref[...])
    hm = (g * _sigmoid(g) * u).astype(BF16)
    o_ref[...] = _dot(hm, wd_ref[...]).astype(o_ref.dtype)


def _shared_expert(h, wg, wu, wd, *, tm=512):
    t, d = h.shape
    f = wg.shape[1]
    tm = min(tm, t)
    return pl.pallas_call(
        _shared_kernel,
        grid=(t // tm,),
        in_specs=[pl.BlockSpec((tm, d), lambda i: (i, 0)),
                  pl.BlockSpec((d, f), lambda i: (0, 0)),
                  pl.BlockSpec((d, f), lambda i: (0, 0)),
                  pl.BlockSpec((f, d), lambda i: (0, 0))],
        out_specs=pl.BlockSpec((tm, d), lambda i: (i, 0)),
        out_shape=jax.ShapeDtypeStruct((t, d), BF16),
        compiler_params=_params(1),
        name="shared_expert",
    )(h, wg, wu, wd)


def _combine_ln_kernel(pos_ref, wts_ref, ys_ref, h_ref, g_ref, b_ref, ye_hbm, o_ref, buf, sem, *, tc):
    def row_copy(src_row, k, n):
        return pltpu.make_async_copy(ye_hbm.at[pl.ds(src_row, 1)], buf.at[k, pl.ds(n, 1)], sem)

    def issue(n, carry):
        for k in range(TOP_K):
            row_copy(pos_ref[n * TOP_K + k], k, n).start()
        return carry

    def drain(n, carry):
        for k in range(TOP_K):
            row_copy(0, k, n).wait()
        return carry

    lax.fori_loop(0, tc, issue, 0)
    lax.fori_loop(0, tc, drain, 0)

    w = wts_ref[...]
    y = ys_ref[...].astype(F32)
    for k in range(TOP_K):
        y = y + w[:, k:k + 1] * buf[k]
    z = DEEPNORM_ALPHA * h_ref[...].astype(F32) + y
    mu = jnp.mean(z, axis=-1, keepdims=True)
    zc = z - mu
    var = jnp.mean(zc * zc, axis=-1, keepdims=True)
    o_ref[...] = (zc * lax.rsqrt(var + LN_EPS) * g_ref[...] + b_ref[...]).astype(o_ref.dtype)


def _combine_ln(pos_flat, wts, ys, h, ye, ln_g, ln_b, *, tc=128):
    t, d = h.shape
    tc = min(tc, t)
    return pl.pallas_call(
        functools.partial(_combine_ln_kernel, tc=tc),
        grid=(t // tc,),
        in_specs=[pl.BlockSpec((tc * TOP_K,), lambda i: (i,), memory_space=pltpu.SMEM),
                  pl.BlockSpec((tc, TOP_K), lambda i: (i, 0)),
                  pl.BlockSpec((tc, d), lambda i: (i, 0)),
                  pl.BlockSpec((tc, d), lambda i: (i, 0)),
                  pl.BlockSpec((1, d), lambda i: (0, 0)),
                  pl.BlockSpec((1, d), lambda i: (0, 0)),
                  pl.BlockSpec(memory_space=pl.ANY)],
        out_specs=pl.BlockSpec((tc, d), lambda i: (i, 0)),
        out_shape=jax.ShapeDtypeStruct((t, d), BF16),
        scratch_shapes=[pltpu.VMEM((TOP_K, tc, d), ye.dtype), pltpu.SemaphoreType.DMA(())],
        compiler_params=_params(1),
        name="combine_ln",
    )(pos_flat, wts, ys, h, ln_g.reshape(1, d), ln_b.reshape(1, d), ye)


def _attn_kernel(q_ref, k_ref, v_ref, o_ref, *, seq, heads, tq, win):
    inv_sqrt_d = 1.0 / math.sqrt(HEAD_DIM)
    jr = lax.broadcasted_iota(I32, (win, win), 0)
    sc = lax.broadcasted_iota(I32, (win, win), 1)
    later = jnp.where(jr > sc, 1.0, 0.0).astype(BF16)
    lane = lax.broadcasted_iota(I32, (tq, win), 1)
    rowi = lax.broadcasted_iota(I32, (tq, win), 0)

    def window(hd, t0, k_start, k_limit, log_surv, acc):
        cols = slice(hd * HEAD_DIM, (hd + 1) * HEAD_DIM)
        ks = pl.multiple_of(k_start, HEAD_DIM)
        qh = q_ref[pl.ds(t0, tq), cols]
        kw = k_ref[pl.ds(ks, win), cols]
        vw = v_ref[pl.ds(ks, win), cols]
        z = lax.dot_general(qh, kw, (((1,), (1,)), ((), ())), preferred_element_type=F32) * inv_sqrt_d
        kpos = k_start + lane
        valid = (kpos < t0 + rowi) & (kpos < k_limit)
        softplus = jnp.maximum(z, 0.0) + jnp.log(1.0 + jnp.exp(-jnp.abs(z)))
        log_fail = jnp.where(valid, -softplus, 0.0)
        log_hit = z - softplus
        lf_hi = log_fail.astype(BF16)
        lf_lo = (log_fail - lf_hi.astype(F32)).astype(BF16)
        between = _dot(lf_hi, later) + _dot(lf_lo, later)
        a = jnp.where(valid, jnp.exp(log_hit + between + log_surv), 0.0)
        acc = acc + _dot(a.astype(BF16), vw)
        log_surv = log_surv + jnp.sum(log_fail, axis=-1, keepdims=True)
        return log_surv, acc

    def q_tile(qi, carry):
        t0 = pl.multiple_of(qi * tq, tq)
        start0 = jnp.maximum(t0 + tq - win, 0)
        surv, accs = [], []
        for hd in range(heads):
            s, a = window(hd, t0, start0, seq + win, jnp.zeros((tq, 1), F32), jnp.zeros((tq, HEAD_DIM), F32))
            surv.append(s)
            accs.append(a)

        def live(surv):
            m = surv[0]
            for s in surv[1:]:
                m = jnp.maximum(m, s)
            return (jnp.max(m) >= EXP_ZERO_BELOW).astype(I32)

        def cond(state):
            prev_start, alive, _, _ = state
            return (prev_start > 0) & (alive > 0)

        def body(state):
            prev_start, _, surv, accs = state
            start = jnp.maximum(prev_start - win, 0)
            new_s, new_a = [], []
            for hd in range(heads):
                s, a = window(hd, t0, start, prev_start, surv[hd], accs[hd])
                new_s.append(s)
                new_a.append(a)
            return start, live(new_s), new_s, new_a

        _, _, _, accs = lax.while_loop(cond, body, (start0, live(surv), surv, accs))
        for hd in range(heads):
            o_ref[pl.ds(t0, tq), hd * HEAD_DIM:(hd + 1) * HEAD_DIM] = accs[hd].astype(o_ref.dtype)
        return carry

    lax.fori_loop(0, seq // tq, q_tile, 0)


def _attention(q, k, v, batch, seq, *, heads_per_step=4, tq=128, win=256):
    t, d = q.shape
    n_heads = d // HEAD_DIM
    hb = min(heads_per_step, n_heads)
    tq, win = min(tq, seq), min(win, seq)
    spec = pl.BlockSpec((seq, hb * HEAD_DIM), lambda b, h: (b, h))
    return pl.pallas_call(
        functools.partial(_attn_kernel, seq=seq, heads=hb, tq=tq, win=win),
        grid=(batch, n_heads // hb),
        in_specs=[spec, spec, spec],
        out_specs=spec,
        out_shape=jax.ShapeDtypeStruct((t, d), BF16),
        compiler_params=_params(2),
        name="attention",
    )(q, k, v)


def _moe_sublayer(h_b, h_f, w_router, bias, w_gate, w_up, w_down, ws_gate, ws_up, ws_down,
                  ln_g, ln_b, *, tile_rows):
    t, d = h_b.shape
    n_exp = w_router.shape[1]
    w_hi = w_router.astype(BF16)
    w_lo = (w_router - w_hi.astype(F32)).astype(BF16)
    idx, wts, rank, cnt = _router(h_b, w_hi, w_lo, bias)

    counts = cnt[0].astype(I32)
    n_tiles_e = (counts + tile_rows - 1) // tile_rows
    tile_end = jnp.cumsum(n_tiles_e)
    tile_start = tile_end - n_tiles_e
    row_off = tile_start * tile_rows
    n_used = tile_end[-1]
    n_tiles = (t * TOP_K) // tile_rows + n_exp
    n_rows = n_tiles * tile_rows

    tile_id = jnp.minimum(jnp.arange(n_tiles, dtype=I32), n_used - 1)
    tile_expert = jnp.searchsorted(tile_end, tile_id, side="right").astype(I32)
    tile_first = ((jnp.arange(n_tiles, dtype=I32) == tile_start[tile_expert])).astype(I32)
    nxt_tile = tile_end[tile_expert]
    nxt_expert = jnp.searchsorted(tile_end, jnp.minimum(nxt_tile, n_used - 1), side="right").astype(I32)
    tile_next = jnp.where(nxt_tile < n_used, nxt_expert, -1).astype(I32)

    pos = _positions(idx, rank, row_off.astype(F32).reshape(1, n_exp))
    pos_flat = pos.reshape(t * TOP_K)
    xs = _dispatch(pos_flat, h_f, (row_off + counts).astype(I32),
                   (row_off + n_tiles_e * tile_rows).astype(I32), n_rows)
    ye = _experts(tile_expert, tile_id, tile_first, tile_next, n_used.reshape(1).astype(I32),
                  xs, w_gate, w_up, w_down, tm=tile_rows)
    ys = _shared_expert(h_b, ws_gate.astype(BF16), ws_up.astype(BF16), ws_down.astype(BF16))
    return _combine_ln(pos_flat, wts, ys, h_b, ye, ln_g, ln_b)


def kernel(x, p, pool_w_in, pool_w_grp, pool_scale, pool_w_out, kv_w_k, kv_w_v, sb_w_q, sb_w_o,
           moe_w_router, moe_bias, moe_w_gate, moe_w_up, moe_w_down, shared_w_gate, shared_w_up,
           shared_w_down, ple_w_proj, ple_w_gate, ple_b_gate, ln_g, ln_b):
    batch, seq, d = x.shape
    t = batch * seq
    xf = x.reshape(t, d)
    pb = p.reshape(p.shape[0], t, p.shape[-1]).astype(BF16)
    bf = lambda w: w.astype(BF16)

    def moe_and_ple(i, h_b, h_f, last):
        h_b = _moe_sublayer(h_b, h_f, moe_w_router[i], moe_bias[i], moe_w_gate[i], moe_w_up[i],
                            moe_w_down[i], shared_w_gate[i], shared_w_up[i], shared_w_down[i],
                            ln_g[i, 1], ln_b[i, 1], tile_rows=EXPERT_TILE_ROWS)
        return _mm_ln(h_b, bf(ple_w_gate[i]), h_b, ln_g[i, 2], ln_b[i, 2],
                      out_dtypes=(F32,) if last else (BF16,),
                      gate=(pb[i], bf(ple_w_proj[i]), ple_b_gate[i]))[0]

    u = _matmul(bf(xf), bf(pool_w_in[0]))
    mixed = _pool_grp(u, bf(pool_w_grp[0]), pool_scale[0], seq)
    h_b, h_f = _mm_ln(mixed, bf(pool_w_out[0]), xf, ln_g[0, 0], ln_b[0, 0], out_dtypes=(BF16, F32))
    h_b = moe_and_ple(0, h_b, h_f, last=False)

    kk = _matmul(h_b, bf(kv_w_k))
    vv = _matmul(h_b, bf(kv_w_v))
    qq = _matmul(h_b, bf(sb_w_q[0]))
    o = _attention(qq, kk, vv, batch, seq)
    h_b, h_f = _mm_ln(o, bf(sb_w_o[0]), h_b, ln_g[1, 0], ln_b[1, 0], out_dtypes=(BF16, F32))
    out = moe_and_ple(1, h_b, h_f, last=True)
    return out.reshape(batch, seq, d)
```

```python
import functools
import math

import jax
import jax.numpy as jnp
from jax import lax
from jax.experimental import pallas as pl
from jax.experimental.pallas import tpu as pltpu

BF16, F32, I32, U32 = jnp.bfloat16, jnp.float32, jnp.int32, jnp.uint32

DEPTH = 2
POOL_WINDOWS = (2, 4, 8, 16)
POOL_HALO = 16
HEAD_DIM = 128
TOP_K = 8
ROUTED_SCALE = 2.5
LN_EPS = 1e-5
DEEPNORM_ALPHA = (2.0 * DEPTH) ** 0.25
EXP_ZERO_BELOW = -110.0
VMEM_LIMIT_BYTES = 56 * 1024 * 1024
EXPERT_TILE_ROWS = 256
PACKED = "packed"


def _params(n_axes=1):
    return pltpu.CompilerParams(dimension_semantics=("arbitrary",) * n_axes,
                                vmem_limit_bytes=VMEM_LIMIT_BYTES)


def _dot(a, b):
    return jnp.dot(a, b, preferred_element_type=F32)


def _sigmoid(x):
    return 1.0 / (1.0 + jnp.exp(-x))


def _pack_bf16_pair(lo, hi):
    lo_bits = lax.bitcast_convert_type(lo.astype(BF16).astype(F32), U32)
    hi_bits = lax.bitcast_convert_type(hi.astype(BF16).astype(F32), U32)
    return hi_bits | (lo_bits >> 16)


def _unpack_bf16_pair(words):
    lo = lax.bitcast_convert_type(words << 16, F32)
    hi = lax.bitcast_convert_type(words & jnp.uint32(0xFFFF0000), F32)
    return lo, hi


def _mm_kernel(x_ref, w_ref, o_ref):
    o_ref[...] = _dot(x_ref[...], w_ref[...]).astype(o_ref.dtype)


def _matmul(x, w, *, tm=1024, tn=1024, out_dtype=BF16):
    m, k = x.shape
    n = w.shape[1]
    tm, tn = min(tm, m), min(tn, n)
    return pl.pallas_call(
        _mm_kernel,
        grid=(m // tm, n // tn),
        in_specs=[pl.BlockSpec((tm, k), lambda i, j: (i, 0)),
                  pl.BlockSpec((k, tn), lambda i, j: (0, j))],
        out_specs=pl.BlockSpec((tm, tn), lambda i, j: (i, j)),
        out_shape=jax.ShapeDtypeStruct((m, n), out_dtype),
        compiler_params=_params(2),
        name="matmul",
    )(x, w)


def _pool_grp_kernel(u_ref, halo_ref, wg_ref, sc_ref, o_ref, *, tm, tiles_per_seq):
    g = pl.program_id(0)
    seq_tile = pl.program_id(1) % tiles_per_seq
    cur = u_ref[...].astype(F32)
    halo = halo_ref[...].astype(F32)
    halo = jnp.where(seq_tile == 0, 0.0, halo)
    ext = jnp.concatenate([halo, cur], axis=0)
    row = lax.broadcasted_iota(I32, (tm, 1), 0)
    pos1 = (seq_tile * tm + row + 1).astype(F32)

    for gi, w in enumerate(POOL_WINDOWS):
        @pl.when(g == gi)
        def _(w=w):
            s = ext
            span = 1
            while span < w:
                s = s + pltpu.roll(s, span, 0)
                span *= 2
            win = s[POOL_HALO:]
            cnt = jnp.minimum(pos1, float(w))
            pooled = win / cnt - cur
            mixed = _dot(pooled.astype(BF16), wg_ref[...]) * sc_ref[...]
            o_ref[...] = mixed.astype(o_ref.dtype)


def _pool_grp(u, w_grp, scale, seq, *, tm=512):
    t, d = u.shape
    ng, dg, _ = w_grp.shape
    tm = min(tm, seq)
    hb = tm // POOL_HALO
    return pl.pallas_call(
        functools.partial(_pool_grp_kernel, tm=tm, tiles_per_seq=seq // tm),
        grid=(ng, t // tm),
        in_specs=[pl.BlockSpec((tm, dg), lambda g, i: (i, g)),
                  pl.BlockSpec((POOL_HALO, dg), lambda g, i: (jnp.maximum(i * hb - 1, 0), g)),
                  pl.BlockSpec((None, dg, dg), lambda g, i: (g, 0, 0)),
                  pl.BlockSpec((1, dg), lambda g, i: (0, g))],
        out_specs=pl.BlockSpec((tm, dg), lambda g, i: (i, g)),
        out_shape=jax.ShapeDtypeStruct((t, d), BF16),
        compiler_params=_params(2),
        name="pool_grp",
    )(u, u, w_grp, scale.reshape(1, d))


def _row_stats_merge(j, tn, z, mean_ref, m2_ref):
    cm = jnp.mean(z, axis=-1, keepdims=True)
    dz = z - cm
    cm2 = jnp.sum(dz * dz, axis=-1, keepdims=True)

    @pl.when(j == 0)
    def _():
        mean_ref[...] = cm
        m2_ref[...] = cm2

    @pl.when(j > 0)
    def _():
        n_a = (j * tn).astype(F32)
        tot = n_a + float(tn)
        delta = cm - mean_ref[...]
        mean_ref[...] = mean_ref[...] + delta * (float(tn) / tot)
        m2_ref[...] = m2_ref[...] + cm2 + delta * delta * (n_a * float(tn) / tot)


def _mm_ln_kernel(*refs, nj, tn, d, gated, out_kinds):
    if gated:
        lhs_ref, w_ref, res_ref, g_ref, b_ref, p_ref, wp_ref, bg_ref = refs[:8]
        rest = refs[8:]
    else:
        lhs_ref, w_ref, res_ref, g_ref, b_ref = refs[:5]
        rest = refs[5:]
    n_out = len(out_kinds)
    outs, (acc_ref, mean_ref, m2_ref) = rest[:n_out], rest[n_out:]
    j = pl.program_id(1)

    y = _dot(lhs_ref[...], w_ref[...])
    if gated:
        y = _sigmoid(y + bg_ref[...]) * _dot(p_ref[...], wp_ref[...])
    z = DEEPNORM_ALPHA * res_ref[...].astype(F32) + y
    acc_ref[j] = z
    _row_stats_merge(j, tn, z, mean_ref, m2_ref)

    @pl.when(j == nj - 1)
    def _():
        mu = mean_ref[...]
        rstd = lax.rsqrt(m2_ref[...] * (1.0 / d) + LN_EPS)

        def normed(jj):
            sl = slice(jj * tn, (jj + 1) * tn)
            return (acc_ref[jj] - mu) * rstd * g_ref[:, sl] + b_ref[:, sl]

        half = nj // 2
        for jj in range(half):
            pair = (jj, jj + half)
            hn = [normed(c) for c in pair]
            for kind, o_ref in zip(out_kinds, outs):
                if kind == PACKED:
                    o_ref[:, jj * tn:(jj + 1) * tn] = _pack_bf16_pair(hn[0], hn[1])
                else:
                    for c, v in zip(pair, hn):
                        o_ref[:, c * tn:(c + 1) * tn] = v.astype(o_ref.dtype)


def _mm_ln(lhs, w, res, ln_g, ln_b, *, out_kinds=(BF16,), gate=None, tm=512, tn=512):
    t, k = lhs.shape
    d = w.shape[1]
    tm, tn = min(tm, t), min(tn, d // 2)
    nj = d // tn
    in_specs = [pl.BlockSpec((tm, k), lambda i, j: (i, 0)),
                pl.BlockSpec((k, tn), lambda i, j: (0, j)),
                pl.BlockSpec((tm, tn), lambda i, j: (i, j)),
                pl.BlockSpec((1, d), lambda i, j: (0, 0)),
                pl.BlockSpec((1, d), lambda i, j: (0, 0))]
    args = [lhs, w, res, ln_g.reshape(1, d), ln_b.reshape(1, d)]
    if gate is not None:
        p, w_proj, b_gate = gate
        pd = p.shape[1]
        in_specs += [pl.BlockSpec((tm, pd), lambda i, j: (i, 0)),
                     pl.BlockSpec((pd, tn), lambda i, j: (0, j)),
                     pl.BlockSpec((1, tn), lambda i, j: (0, j))]
        args += [p, w_proj, b_gate.reshape(1, d)]
    out_shapes = [jax.ShapeDtypeStruct((t, d // 2), U32) if kind == PACKED
                  else jax.ShapeDtypeStruct((t, d), kind) for kind in out_kinds]
    return pl.pallas_call(
        functools.partial(_mm_ln_kernel, nj=nj, tn=tn, d=d, gated=gate is not None,
                          out_kinds=tuple(out_kinds)),
        grid=(t // tm, nj),
        in_specs=in_specs,
        out_specs=[pl.BlockSpec((tm, s.shape[1]), lambda i, j: (i, 0)) for s in out_shapes],
        out_shape=out_shapes,
        scratch_shapes=[pltpu.VMEM((nj, tm, tn), F32),
                        pltpu.VMEM((tm, 1), F32),
                        pltpu.VMEM((tm, 1), F32)],
        compiler_params=_params(2),
        name="mm_ln_gated" if gate is not None else "mm_ln",
    )(*args)


def _router_kernel(h_ref, whi_ref, wlo_ref, bias_ref, idx_ref, wts_ref, rank_ref, cnt_ref,
                   carry_ref, *, tm, n_exp):
    @pl.when(pl.program_id(0) == 0)
    def _():
        carry_ref[...] = jnp.zeros_like(carry_ref)

    h = h_ref[...]
    scores = _sigmoid(_dot(h, whi_ref[...]) + _dot(h, wlo_ref[...]))
    sel = scores + bias_ref[...]
    lane = lax.broadcasted_iota(I32, (tm, n_exp), 1)
    lane_k = lax.broadcasted_iota(I32, (tm, TOP_K), 1)
    mask = jnp.zeros((tm, n_exp), F32)
    idxs = jnp.zeros((tm, TOP_K), I32)
    wsel = jnp.zeros((tm, TOP_K), F32)
    for k in range(TOP_K):
        m = jnp.max(sel, axis=-1, keepdims=True)
        ik = jnp.min(jnp.where(sel == m, lane, n_exp), axis=-1, keepdims=True)
        onehot = lane == ik
        mask = jnp.where(onehot, 1.0, mask)
        sel = jnp.where(onehot, -jnp.inf, sel)
        sk = jnp.sum(jnp.where(onehot, scores, 0.0), axis=-1, keepdims=True)
        idxs = jnp.where(lane_k == k, ik, idxs)
        wsel = jnp.where(lane_k == k, sk, wsel)
    idx_ref[...] = idxs
    wts_ref[...] = wsel / jnp.sum(wsel, axis=-1, keepdims=True) * ROUTED_SCALE

    r = lax.broadcasted_iota(I32, (tm, tm), 0)
    c = lax.broadcasted_iota(I32, (tm, tm), 1)
    lower = jnp.where(c < r, 1.0, 0.0).astype(BF16)
    rank_ref[...] = carry_ref[...] + _dot(lower, mask.astype(BF16))
    carry_ref[...] = carry_ref[...] + jnp.sum(mask, axis=0, keepdims=True)
    cnt_ref[...] = carry_ref[...]


def _router(h, w_hi, w_lo, bias, *, tm=512):
    t, d = h.shape
    n_exp = w_hi.shape[1]
    tm = min(tm, t)
    return pl.pallas_call(
        functools.partial(_router_kernel, tm=tm, n_exp=n_exp),
        grid=(t // tm,),
        in_specs=[pl.BlockSpec((tm, d), lambda i: (i, 0)),
                  pl.BlockSpec((d, n_exp), lambda i: (0, 0)),
                  pl.BlockSpec((d, n_exp), lambda i: (0, 0)),
                  pl.BlockSpec((1, n_exp), lambda i: (0, 0))],
        out_specs=[pl.BlockSpec((tm, TOP_K), lambda i: (i, 0)),
                   pl.BlockSpec((tm, TOP_K), lambda i: (i, 0)),
                   pl.BlockSpec((tm, n_exp), lambda i: (i, 0)),
                   pl.BlockSpec((1, n_exp), lambda i: (0, 0))],
        out_shape=[jax.ShapeDtypeStruct((t, TOP_K), I32),
                   jax.ShapeDtypeStruct((t, TOP_K), F32),
                   jax.ShapeDtypeStruct((t, n_exp), F32),
                   jax.ShapeDtypeStruct((1, n_exp), F32)],
        scratch_shapes=[pltpu.VMEM((1, n_exp), F32)],
        compiler_params=_params(1),
        name="router",
    )(h, w_hi, w_lo, bias.reshape(1, n_exp))


def _pos_kernel(idx_ref, rank_ref, off_ref, pos_ref, *, tm, n_exp):
    base = rank_ref[...] + off_ref[...]
    idx = idx_ref[...]
    lane = lax.broadcasted_iota(I32, (tm, n_exp), 1)
    lane_k = lax.broadcasted_iota(I32, (tm, TOP_K), 1)
    pos = jnp.zeros((tm, TOP_K), F32)
    for k in range(TOP_K):
        pk = jnp.sum(jnp.where(lane == idx[:, k:k + 1], base, 0.0), axis=-1, keepdims=True)
        pos = jnp.where(lane_k == k, pk, pos)
    pos_ref[...] = pos.astype(I32)


def _positions(idx, rank, row_off, *, tm=512):
    t, n_exp = rank.shape
    tm = min(tm, t)
    return pl.pallas_call(
        functools.partial(_pos_kernel, tm=tm, n_exp=n_exp),
        grid=(t // tm,),
        in_specs=[pl.BlockSpec((tm, TOP_K), lambda i: (i, 0)),
                  pl.BlockSpec((tm, n_exp), lambda i: (i, 0)),
                  pl.BlockSpec((1, n_exp), lambda i: (0, 0))],
        out_specs=pl.BlockSpec((tm, TOP_K), lambda i: (i, 0)),
        out_shape=jax.ShapeDtypeStruct((t, TOP_K), I32),
        compiler_params=_params(1),
        name="positions",
    )(idx, rank, row_off)


def _dispatch_kernel(pad_lo_ref, pad_hi_ref, pos_ref, h_ref, xs_ref, zero_ref, sem, *, td, n_exp):
    i = pl.program_id(0)

    def row_copy(src_ref, src_row, dst_row, s):
        return pltpu.make_async_copy(src_ref.at[pl.ds(src_row, 1)], xs_ref.at[pl.ds(dst_row, 1)], sem.at[s])

    def issue(n, carry):
        for k in range(TOP_K):
            row_copy(h_ref, n, pos_ref[n * TOP_K + k], 0).start()
        return carry

    def drain(n, carry):
        for k in range(TOP_K):
            row_copy(h_ref, 0, 0, 0).wait()
        return carry

    lax.fori_loop(0, td, issue, 0)

    @pl.when(i == 0)
    def _():
        zero_ref[...] = jnp.zeros_like(zero_ref)

        def per_expert(e, carry):
            lo, hi = pad_lo_ref[e], pad_hi_ref[e]

            def zissue(r, c):
                row_copy(zero_ref, 0, r, 1).start()
                return c

            def zdrain(r, c):
                row_copy(zero_ref, 0, 0, 1).wait()
                return c

            lax.fori_loop(lo, hi, zissue, 0)
            lax.fori_loop(lo, hi, zdrain, 0)
            return carry

        lax.fori_loop(0, n_exp, per_expert, 0)

    lax.fori_loop(0, td, drain, 0)


def _dispatch(pos_flat, h_packed, pad_lo, pad_hi, n_rows, *, td=256):
    t, dh = h_packed.shape
    td = min(td, t)
    n_exp = pad_lo.shape[0]
    grid_spec = pltpu.PrefetchScalarGridSpec(
        num_scalar_prefetch=2,
        grid=(t // td,),
        in_specs=[pl.BlockSpec((td * TOP_K,), lambda i, lo, hi: (i,), memory_space=pltpu.SMEM),
                  pl.BlockSpec((td, dh), lambda i, lo, hi: (i, 0))],
        out_specs=pl.BlockSpec(memory_space=pl.ANY),
        scratch_shapes=[pltpu.VMEM((8, dh), h_packed.dtype), pltpu.SemaphoreType.DMA((2,))],
    )
    return pl.pallas_call(
        functools.partial(_dispatch_kernel, td=td, n_exp=n_exp),
        grid_spec=grid_spec,
        out_shape=jax.ShapeDtypeStruct((n_rows, dh), h_packed.dtype),
        compiler_params=pltpu.CompilerParams(dimension_semantics=("arbitrary",),
                                             vmem_limit_bytes=VMEM_LIMIT_BYTES,
                                             has_side_effects=True),
        name="dispatch",
    )(pad_lo, pad_hi, pos_flat, h_packed)


def _expert_kernel(te_ref, ts_ref, first_ref, nxt_ref, nused_ref,
                   xs_ref, wg_hbm, wu_hbm, wd_hbm, ye_ref,
                   wg_f, wu_f, wd_f, wgu_b, wd_b, sem, *, layer, f):
    j = pl.program_id(0)

    def weight_copies(e):
        return (pltpu.make_async_copy(wg_hbm.at[layer, e], wg_f, sem.at[0]),
                pltpu.make_async_copy(wu_hbm.at[layer, e], wu_f, sem.at[1]),
                pltpu.make_async_copy(wd_hbm.at[layer, e], wd_f, sem.at[2]))

    @pl.when(j == 0)
    def _():
        for cp in weight_copies(te_ref[0]):
            cp.start()

    @pl.when(first_ref[j] == 1)
    def _():
        for cp in weight_copies(te_ref[j]):
            cp.wait()
        wgu_b[:, :f] = wg_f[...].astype(BF16)
        wgu_b[:, f:] = wu_f[...].astype(BF16)
        wd_b[...] = wd_f[...].astype(BF16)

        @pl.when(nxt_ref[j] >= 0)
        def _():
            for cp in weight_copies(nxt_ref[j]):
                cp.start()

    @pl.when(j < nused_ref[0])
    def _():
        lo, hi = _unpack_bf16_pair(xs_ref[...])
        x = jnp.concatenate([lo.astype(BF16), hi.astype(BF16)], axis=1)
        gu = _dot(x, wgu_b[...])
        g, u = gu[:, :f], gu[:, f:]
        hm = (g * _sigmoid(g) * u).astype(BF16)
        out = _dot(hm, wd_b[...])
        dh = out.shape[1] // 2
        ye_ref[...] = _pack_bf16_pair(out[:, :dh], out[:, dh:])


def _experts(tile_expert, tile_src, tile_first, tile_next, n_used, xs, w_gate, w_up, w_down,
             *, layer, tm):
    n_rows, dh = xs.shape
    d, f = w_gate.shape[2], w_gate.shape[3]
    grid_spec = pltpu.PrefetchScalarGridSpec(
        num_scalar_prefetch=5,
        grid=(n_rows // tm,),
        in_specs=[pl.BlockSpec((tm, dh), lambda j, te, ts, fi, nx, nu: (ts[j], 0)),
                  pl.BlockSpec(memory_space=pl.ANY),
                  pl.BlockSpec(memory_space=pl.ANY),
                  pl.BlockSpec(memory_space=pl.ANY)],
        out_specs=pl.BlockSpec((tm, dh), lambda j, te, ts, fi, nx, nu: (ts[j], 0)),
        scratch_shapes=[pltpu.VMEM((d, f), F32), pltpu.VMEM((d, f), F32), pltpu.VMEM((f, d), F32),
                        pltpu.VMEM((d, 2 * f), BF16), pltpu.VMEM((f, d), BF16),
                        pltpu.SemaphoreType.DMA((3,))],
    )
    return pl.pallas_call(
        functools.partial(_expert_kernel, layer=layer, f=f),
        grid_spec=grid_spec,
        out_shape=jax.ShapeDtypeStruct((n_rows, dh), xs.dtype),
        compiler_params=_params(1),
        name="experts",
    )(tile_expert, tile_src, tile_first, tile_next, n_used, xs, w_gate, w_up, w_down)


def _shared_kernel(h_ref, wg_ref, wu_ref, wd_ref, o_ref):
    h = h_ref[...]
    g = _dot(h, wg_ref[...])
    u = _dot(h, wu_ref[...])
    hm = (g * _sigmoid(g) * u).astype(BF16)
    o_ref[...] = _dot(hm, wd_ref[...]).astype(o_ref.dtype)


def _shared_expert(h, wg, wu, wd, *, tm=512):
    t, d = h.shape
    f = wg.shape[1]
    tm = min(tm, t)
    return pl.pallas_call(
        _shared_kernel,
        grid=(t // tm,),
        in_specs=[pl.BlockSpec((tm, d), lambda i: (i, 0)),
                  pl.BlockSpec((d, f), lambda i: (0, 0)),
                  pl.BlockSpec((d, f), lambda i: (0, 0)),
                  pl.BlockSpec((f, d), lambda i: (0, 0))],
        out_specs=pl.BlockSpec((tm, d), lambda i: (i, 0)),
        out_shape=jax.ShapeDtypeStruct((t, d), BF16),
        compiler_params=_params(1),
        name="shared_expert",
    )(h, wg, wu, wd)


def _combine_ln_kernel(pos_ref, wts_ref, ys_ref, h_ref, g_ref, b_ref, ye_hbm, o_ref, buf, sem, *, tc):
    def row_copy(src_row, k, n):
        return pltpu.make_async_copy(ye_hbm.at[pl.ds(src_row, 1)], buf.at[k, pl.ds(n, 1)], sem)

    def issue(n, carry):
        for k in range(TOP_K):
            row_copy(pos_ref[n * TOP_K + k], k, n).start()
        return carry

    def drain(n, carry):
        for k in range(TOP_K):
            row_copy(0, k, n).wait()
        return carry

    lax.fori_loop(0, tc, issue, 0)
    lax.fori_loop(0, tc, drain, 0)

    dh = buf.shape[2]
    d = 2 * dh
    w = wts_ref[...]
    y_lo = ys_ref[:, :dh].astype(F32)
    y_hi = ys_ref[:, dh:].astype(F32)
    for k in range(TOP_K):
        lo, hi = _unpack_bf16_pair(buf[k])
        y_lo = y_lo + w[:, k:k + 1] * lo
        y_hi = y_hi + w[:, k:k + 1] * hi
    z_lo = DEEPNORM_ALPHA * h_ref[:, :dh].astype(F32) + y_lo
    z_hi = DEEPNORM_ALPHA * h_ref[:, dh:].astype(F32) + y_hi
    mu = (jnp.sum(z_lo, axis=-1, keepdims=True) + jnp.sum(z_hi, axis=-1, keepdims=True)) * (1.0 / d)
    c_lo, c_hi = z_lo - mu, z_hi - mu
    var = (jnp.sum(c_lo * c_lo, axis=-1, keepdims=True)
           + jnp.sum(c_hi * c_hi, axis=-1, keepdims=True)) * (1.0 / d)
    rstd = lax.rsqrt(var + LN_EPS)
    o_ref[:, :dh] = (c_lo * rstd * g_ref[:, :dh] + b_ref[:, :dh]).astype(o_ref.dtype)
    o_ref[:, dh:] = (c_hi * rstd * g_ref[:, dh:] + b_ref[:, dh:]).astype(o_ref.dtype)


def _combine_ln(pos_flat, wts, ys, h, ye, ln_g, ln_b, *, tc=128):
    t, d = h.shape
    tc = min(tc, t)
    return pl.pallas_call(
        functools.partial(_combine_ln_kernel, tc=tc),
        grid=(t // tc,),
        in_specs=[pl.BlockSpec((tc * TOP_K,), lambda i: (i,), memory_space=pltpu.SMEM),
                  pl.BlockSpec((tc, TOP_K), lambda i: (i, 0)),
                  pl.BlockSpec((tc, d), lambda i: (i, 0)),
                  pl.BlockSpec((tc, d), lambda i: (i, 0)),
                  pl.BlockSpec((1, d), lambda i: (0, 0)),
                  pl.BlockSpec((1, d), lambda i: (0, 0)),
                  pl.BlockSpec(memory_space=pl.ANY)],
        out_specs=pl.BlockSpec((tc, d), lambda i: (i, 0)),
        out_shape=jax.ShapeDtypeStruct((t, d), BF16),
        scratch_shapes=[pltpu.VMEM((TOP_K, tc, d // 2), ye.dtype), pltpu.SemaphoreType.DMA(())],
        compiler_params=_params(1),
        name="combine_ln",
    )(pos_flat, wts, ys, h, ln_g.reshape(1, d), ln_b.reshape(1, d), ye)


def _later_matrix(win):
    j = lax.broadcasted_iota(I32, (win, win), 0)
    s = lax.broadcasted_iota(I32, (win, win), 1)
    return jnp.where(j > s, 1.0, 0.0).astype(BF16)


def _attn_kernel(q_ref, k_ref, v_ref, o_ref, *, seq, heads, tq, win0, win):
    inv_sqrt_d = 1.0 / math.sqrt(HEAD_DIM)
    later = {w: _later_matrix(w) for w in {win0, win}}

    def add_window(w, t0, k_start, k_limit, surv, accs):
        ks = pl.multiple_of(k_start, HEAD_DIM)
        lane = lax.broadcasted_iota(I32, (tq, w), 1)
        rowi = lax.broadcasted_iota(I32, (tq, w), 0)
        kpos = k_start + lane
        valid = (kpos < t0 + rowi) & (kpos < k_limit)
        cols = [slice(h * HEAD_DIM, (h + 1) * HEAD_DIM) for h in range(heads)]
        z = [lax.dot_general(q_ref[pl.ds(t0, tq), c], k_ref[pl.ds(ks, w), c],
                             (((1,), (1,)), ((), ())), preferred_element_type=F32) * inv_sqrt_d
             for c in cols]
        softplus = [jnp.maximum(x, 0.0) + jnp.log(1.0 + jnp.exp(-jnp.abs(x))) for x in z]
        log_fail = [jnp.where(valid, -sp, 0.0) for sp in softplus]
        lf_hi = [lf.astype(BF16) for lf in log_fail]
        lf_lo = [(lf - hi.astype(F32)).astype(BF16) for lf, hi in zip(log_fail, lf_hi)]
        between = [_dot(hi, later[w]) + _dot(lo, later[w]) for hi, lo in zip(lf_hi, lf_lo)]
        a = [jnp.where(valid, jnp.exp(x - sp + b + s), 0.0)
             for x, sp, b, s in zip(z, softplus, between, surv)]
        accs = [acc + _dot(p.astype(BF16), v_ref[pl.ds(ks, w), c]) for acc, p, c in zip(accs, a, cols)]
        surv = [s + jnp.sum(lf, axis=-1, keepdims=True) for s, lf in zip(surv, log_fail)]
        return surv, accs

    def any_alive(surv):
        m = surv[0]
        for s in surv[1:]:
            m = jnp.maximum(m, s)
        return (jnp.max(m) >= EXP_ZERO_BELOW).astype(I32)

    def q_tile(qi, carry):
        t0 = pl.multiple_of(qi * tq, tq)
        start0 = jnp.maximum(t0 + tq - win0, 0)
        surv = [jnp.zeros((tq, 1), F32) for _ in range(heads)]
        accs = [jnp.zeros((tq, HEAD_DIM), F32) for _ in range(heads)]
        surv, accs = add_window(win0, t0, start0, seq + win0, surv, accs)

        def cond(state):
            prev_start, alive, _, _ = state
            return (prev_start > 0) & (alive > 0)

        def body(state):
            prev_start, _, surv, accs = state
            start = jnp.maximum(prev_start - win, 0)
            surv, accs = add_window(win, t0, start, prev_start, surv, accs)
            return start, any_alive(surv), surv, accs

        _, _, _, accs = lax.while_loop(cond, body, (start0, any_alive(surv), surv, accs))
        for h in range(heads):
            o_ref[pl.ds(t0, tq), h * HEAD_DIM:(h + 1) * HEAD_DIM] = accs[h].astype(o_ref.dtype)
        return carry

    lax.fori_loop(0, seq // tq, q_tile, 0)


def _attention(q, k, v, batch, seq, *, heads_per_step=4, tq=128, win0=384, win=256):
    t, d = q.shape
    n_heads = d // HEAD_DIM
    hb = min(heads_per_step, n_heads)
    tq, win0, win = min(tq, seq), min(win0, seq), min(win, seq)
    spec = pl.BlockSpec((seq, hb * HEAD_DIM), lambda b, h: (b, h))
    return pl.pallas_call(
        functools.partial(_attn_kernel, seq=seq, heads=hb, tq=tq, win0=win0, win=win),
        grid=(batch, n_heads // hb),
        in_specs=[spec, spec, spec],
        out_specs=spec,
        out_shape=jax.ShapeDtypeStruct((t, d), BF16),
        compiler_params=_params(2),
        name="attention",
    )(q, k, v)


def _moe_sublayer(layer, h_b, h_packed, w_router, bias, w_gate, w_up, w_down, ws_gate, ws_up, ws_down,
                  ln_g, ln_b, *, tile_rows):
    t, d = h_b.shape
    n_exp = w_router.shape[1]
    w_hi = w_router.astype(BF16)
    w_lo = (w_router - w_hi.astype(F32)).astype(BF16)
    idx, wts, rank, cnt = _router(h_b, w_hi, w_lo, bias)

    counts = cnt[0].astype(I32)
    n_tiles_e = (counts + tile_rows - 1) // tile_rows
    tile_end = jnp.cumsum(n_tiles_e)
    tile_start = tile_end - n_tiles_e
    row_off = tile_start * tile_rows
    n_used = tile_end[-1]
    n_tiles = (t * TOP_K) // tile_rows + n_exp
    n_rows = n_tiles * tile_rows

    def expert_of(tile):
        return jnp.sum((tile_end[None, :] <= tile[:, None]).astype(I32), axis=1)

    def lookup(table, e):
        onehot = e[:, None] == jnp.arange(n_exp, dtype=I32)[None, :]
        return jnp.sum(jnp.where(onehot, table[None, :], 0), axis=1)

    tile_raw = jnp.arange(n_tiles, dtype=I32)
    tile_id = jnp.minimum(tile_raw, n_used - 1)
    tile_expert = expert_of(tile_id)
    tile_first = (tile_raw == lookup(tile_start, tile_expert)).astype(I32)
    nxt_tile = lookup(tile_end, tile_expert)
    tile_next = jnp.where(nxt_tile < n_used, expert_of(jnp.minimum(nxt_tile, n_used - 1)), -1).astype(I32)

    pos = _positions(idx, rank, row_off.astype(F32).reshape(1, n_exp))
    pos_flat = pos.reshape(t * TOP_K)
    xs = _dispatch(pos_flat, h_packed, (row_off + counts).astype(I32),
                   (row_off + n_tiles_e * tile_rows).astype(I32), n_rows)
    ye = _experts(tile_expert, tile_id, tile_first, tile_next, n_used.reshape(1).astype(I32),
                  xs, w_gate, w_up, w_down, layer=layer, tm=tile_rows)
    ys = _shared_expert(h_b, ws_gate.astype(BF16), ws_up.astype(BF16), ws_down.astype(BF16))
    return _combine_ln(pos_flat, wts, ys, h_b, ye, ln_g, ln_b)


def kernel(x, p, pool_w_in, pool_w_grp, pool_scale, pool_w_out, kv_w_k, kv_w_v, sb_w_q, sb_w_o,
           moe_w_router, moe_bias, moe_w_gate, moe_w_up, moe_w_down, shared_w_gate, shared_w_up,
           shared_w_down, ple_w_proj, ple_w_gate, ple_b_gate, ln_g, ln_b):
    batch, seq, d = x.shape
    t = batch * seq
    xf = x.reshape(t, d)
    pb = p.reshape(p.shape[0], t, p.shape[-1]).astype(BF16)
    bf = lambda w: w.astype(BF16)

    def moe_and_ple(i, h_b, h_packed, last):
        h_b = _moe_sublayer(i, h_b, h_packed, moe_w_router[i], moe_bias[i], moe_w_gate, moe_w_up,
                            moe_w_down, shared_w_gate[i], shared_w_up[i], shared_w_down[i],
                            ln_g[i, 1], ln_b[i, 1], tile_rows=EXPERT_TILE_ROWS)
        return _mm_ln(h_b, bf(ple_w_gate[i]), h_b, ln_g[i, 2], ln_b[i, 2],
                      out_kinds=(F32,) if last else (BF16,),
                      gate=(pb[i], bf(ple_w_proj[i]), ple_b_gate[i]))[0]

    u = _matmul(bf(xf), bf(pool_w_in[0]))
    mixed = _pool_grp(u, bf(pool_w_grp[0]), pool_scale[0], seq)
    h_b, h_packed = _mm_ln(mixed, bf(pool_w_out[0]), xf, ln_g[0, 0], ln_b[0, 0], out_kinds=(BF16, PACKED))
    h_b = moe_and_ple(0, h_b, h_packed, last=False)

    kk = _matmul(h_b, bf(kv_w_k))
    vv = _matmul(h_b, bf(kv_w_v))
    qq = _matmul(h_b, bf(sb_w_q[0]))
    o = _attention(qq, kk, vv, batch, seq)
    h_b, h_packed = _mm_ln(o, bf(sb_w_o[0]), h_b, ln_g[1, 0], ln_b[1, 0], out_kinds=(BF16, PACKED))
    out = moe_and_ple(1, h_b, h_packed, last=True)
    return out.reshape(batch, seq, d)
```

```python
import functools
import math

import jax
import jax.numpy as jnp
from jax import lax
from jax.experimental import pallas as pl
from jax.experimental.pallas import tpu as pltpu

BF16, F32, I32, U32 = jnp.bfloat16, jnp.float32, jnp.int32, jnp.uint32

DEPTH = 2
POOL_WINDOWS = (2, 4, 8, 16)
POOL_HALO = 16
HEAD_DIM = 128
TOP_K = 8
ROUTED_SCALE = 2.5
LN_EPS = 1e-5
DEEPNORM_ALPHA = (2.0 * DEPTH) ** 0.25
EXP_ZERO_BELOW = -110.0
VMEM_LIMIT_BYTES = 56 * 1024 * 1024
EXPERT_TILE_ROWS = 256
EXPERT_CHUNKS = 4
MXU_COLS = 256
PACKED = "packed"


def _params(n_axes=1):
    return pltpu.CompilerParams(dimension_semantics=("arbitrary",) * n_axes,
                                vmem_limit_bytes=VMEM_LIMIT_BYTES)


def _round_up(x, m):
    return (x + m - 1) // m * m


def _dot(a, b):
    return jnp.dot(a, b, preferred_element_type=F32)


def _sigmoid(x):
    return 1.0 / (1.0 + jnp.exp(-x))


def _pack_bf16_pair(lo, hi):
    lo_bits = lax.bitcast_convert_type(lo.astype(BF16).astype(F32), U32)
    hi_bits = lax.bitcast_convert_type(hi.astype(BF16).astype(F32), U32)
    return hi_bits | (lo_bits >> 16)


def _unpack_bf16_pair(words):
    lo = lax.bitcast_convert_type(words << 16, F32)
    hi = lax.bitcast_convert_type(words & jnp.uint32(0xFFFF0000), F32)
    return lo, hi


def _mm_kernel(x_ref, w_ref, o_ref):
    o_ref[...] = _dot(x_ref[...], w_ref[...]).astype(o_ref.dtype)


def _matmul(x, w, *, tm=1024, tn=1024, out_dtype=BF16):
    m, k = x.shape
    n = w.shape[1]
    tm, tn = min(tm, m), min(tn, n)
    return pl.pallas_call(
        _mm_kernel,
        grid=(m // tm, n // tn),
        in_specs=[pl.BlockSpec((tm, k), lambda i, j: (i, 0)),
                  pl.BlockSpec((k, tn), lambda i, j: (0, j))],
        out_specs=pl.BlockSpec((tm, tn), lambda i, j: (i, j)),
        out_shape=jax.ShapeDtypeStruct((m, n), out_dtype),
        compiler_params=_params(2),
        name="matmul",
    )(x, w)


def _pool_grp_kernel(u_ref, halo_ref, wg_ref, sc_ref, o_ref, *, tm, tiles_per_seq):
    g = pl.program_id(0)
    seq_tile = pl.program_id(1) % tiles_per_seq
    cur = u_ref[...].astype(F32)
    halo = halo_ref[...].astype(F32)
    halo = jnp.where(seq_tile == 0, 0.0, halo)
    ext = jnp.concatenate([halo, cur], axis=0)
    row = lax.broadcasted_iota(I32, (tm, 1), 0)
    pos1 = (seq_tile * tm + row + 1).astype(F32)

    for gi, w in enumerate(POOL_WINDOWS):
        @pl.when(g == gi)
        def _(w=w):
            s = ext
            span = 1
            while span < w:
                s = s + pltpu.roll(s, span, 0)
                span *= 2
            win = s[POOL_HALO:]
            cnt = jnp.minimum(pos1, float(w))
            pooled = win / cnt - cur
            mixed = _dot(pooled.astype(BF16), wg_ref[...]) * sc_ref[...]
            o_ref[...] = mixed.astype(o_ref.dtype)


def _pool_grp(u, w_grp, scale, seq, *, tm=512):
    t, d = u.shape
    ng, dg, _ = w_grp.shape
    tm = min(tm, seq)
    hb = tm // POOL_HALO
    return pl.pallas_call(
        functools.partial(_pool_grp_kernel, tm=tm, tiles_per_seq=seq // tm),
        grid=(ng, t // tm),
        in_specs=[pl.BlockSpec((tm, dg), lambda g, i: (i, g)),
                  pl.BlockSpec((POOL_HALO, dg), lambda g, i: (jnp.maximum(i * hb - 1, 0), g)),
                  pl.BlockSpec((None, dg, dg), lambda g, i: (g, 0, 0)),
                  pl.BlockSpec((1, dg), lambda g, i: (0, g))],
        out_specs=pl.BlockSpec((tm, dg), lambda g, i: (i, g)),
        out_shape=jax.ShapeDtypeStruct((t, d), BF16),
        compiler_params=_params(2),
        name="pool_grp",
    )(u, u, w_grp, scale.reshape(1, d))


def _row_stats_merge(j, tn, z, mean_ref, m2_ref):
    cm = jnp.mean(z, axis=-1, keepdims=True)
    dz = z - cm
    cm2 = jnp.sum(dz * dz, axis=-1, keepdims=True)

    @pl.when(j == 0)
    def _():
        mean_ref[...] = cm
        m2_ref[...] = cm2

    @pl.when(j > 0)
    def _():
        n_a = (j * tn).astype(F32)
        tot = n_a + float(tn)
        delta = cm - mean_ref[...]
        mean_ref[...] = mean_ref[...] + delta * (float(tn) / tot)
        m2_ref[...] = m2_ref[...] + cm2 + delta * delta * (n_a * float(tn) / tot)


def _mm_ln_kernel(*refs, nj, tn, d, gated, out_kinds):
    if gated:
        lhs_ref, w_ref, res_ref, g_ref, b_ref, p_ref, wp_ref, bg_ref = refs[:8]
        rest = refs[8:]
    else:
        lhs_ref, w_ref, res_ref, g_ref, b_ref = refs[:5]
        rest = refs[5:]
    n_out = len(out_kinds)
    outs, (acc_ref, mean_ref, m2_ref) = rest[:n_out], rest[n_out:]
    j = pl.program_id(1)

    y = _dot(lhs_ref[...], w_ref[...])
    if gated:
        y = _sigmoid(y + bg_ref[...]) * _dot(p_ref[...], wp_ref[...])
    z = DEEPNORM_ALPHA * res_ref[...].astype(F32) + y
    acc_ref[j] = z
    _row_stats_merge(j, tn, z, mean_ref, m2_ref)

    @pl.when(j == nj - 1)
    def _():
        mu = mean_ref[...]
        rstd = lax.rsqrt(m2_ref[...] * (1.0 / d) + LN_EPS)

        def normed(jj):
            sl = slice(jj * tn, (jj + 1) * tn)
            return (acc_ref[jj] - mu) * rstd * g_ref[:, sl] + b_ref[:, sl]

        half = nj // 2
        for jj in range(half):
            pair = (jj, jj + half)
            hn = [normed(c) for c in pair]
            for kind, o_ref in zip(out_kinds, outs):
                if kind == PACKED:
                    o_ref[:, jj * tn:(jj + 1) * tn] = _pack_bf16_pair(hn[0], hn[1])
                else:
                    for c, v in zip(pair, hn):
                        o_ref[:, c * tn:(c + 1) * tn] = v.astype(o_ref.dtype)


def _mm_ln(lhs, w, res, ln_g, ln_b, *, out_kinds=(BF16,), gate=None, tm=512, tn=512):
    t, k = lhs.shape
    d = w.shape[1]
    tm, tn = min(tm, t), min(tn, d // 2)
    nj = d // tn
    in_specs = [pl.BlockSpec((tm, k), lambda i, j: (i, 0)),
                pl.BlockSpec((k, tn), lambda i, j: (0, j)),
                pl.BlockSpec((tm, tn), lambda i, j: (i, j)),
                pl.BlockSpec((1, d), lambda i, j: (0, 0)),
                pl.BlockSpec((1, d), lambda i, j: (0, 0))]
    args = [lhs, w, res, ln_g.reshape(1, d), ln_b.reshape(1, d)]
    if gate is not None:
        p, w_proj, b_gate = gate
        pd = p.shape[1]
        in_specs += [pl.BlockSpec((tm, pd), lambda i, j: (i, 0)),
                     pl.BlockSpec((pd, tn), lambda i, j: (0, j)),
                     pl.BlockSpec((1, tn), lambda i, j: (0, j))]
        args += [p, w_proj, b_gate.reshape(1, d)]
    out_shapes = [jax.ShapeDtypeStruct((t, d // 2), U32) if kind == PACKED
                  else jax.ShapeDtypeStruct((t, d), kind) for kind in out_kinds]
    return pl.pallas_call(
        functools.partial(_mm_ln_kernel, nj=nj, tn=tn, d=d, gated=gate is not None,
                          out_kinds=tuple(out_kinds)),
        grid=(t // tm, nj),
        in_specs=in_specs,
        out_specs=[pl.BlockSpec((tm, s.shape[1]), lambda i, j: (i, 0)) for s in out_shapes],
        out_shape=out_shapes,
        scratch_shapes=[pltpu.VMEM((nj, tm, tn), F32),
                        pltpu.VMEM((tm, 1), F32),
                        pltpu.VMEM((tm, 1), F32)],
        compiler_params=_params(2),
        name="mm_ln_gated" if gate is not None else "mm_ln",
    )(*args)


def _router_kernel(h_ref, whi_ref, wlo_ref, bias_ref, idx_ref, wts_ref, rank_ref, cnt_ref,
                   carry_ref, *, tm, n_exp):
    @pl.when(pl.program_id(0) == 0)
    def _():
        carry_ref[...] = jnp.zeros_like(carry_ref)

    h = h_ref[...]
    scores = _sigmoid(_dot(h, whi_ref[...]) + _dot(h, wlo_ref[...]))
    sel = scores + bias_ref[...]
    lane = lax.broadcasted_iota(I32, (tm, n_exp), 1)
    lane_k = lax.broadcasted_iota(I32, (tm, TOP_K), 1)
    mask = jnp.zeros((tm, n_exp), F32)
    idxs = jnp.zeros((tm, TOP_K), I32)
    wsel = jnp.zeros((tm, TOP_K), F32)
    for k in range(TOP_K):
        m = jnp.max(sel, axis=-1, keepdims=True)
        ik = jnp.min(jnp.where(sel == m, lane, n_exp), axis=-1, keepdims=True)
        onehot = lane == ik
        mask = jnp.where(onehot, 1.0, mask)
        sel = jnp.where(onehot, -jnp.inf, sel)
        sk = jnp.sum(jnp.where(onehot, scores, 0.0), axis=-1, keepdims=True)
        idxs = jnp.where(lane_k == k, ik, idxs)
        wsel = jnp.where(lane_k == k, sk, wsel)
    idx_ref[...] = idxs
    wts_ref[...] = wsel / jnp.sum(wsel, axis=-1, keepdims=True) * ROUTED_SCALE

    r = lax.broadcasted_iota(I32, (tm, tm), 0)
    c = lax.broadcasted_iota(I32, (tm, tm), 1)
    lower = jnp.where(c < r, 1.0, 0.0).astype(BF16)
    rank_ref[...] = carry_ref[...] + _dot(lower, mask.astype(BF16))
    carry_ref[...] = carry_ref[...] + jnp.sum(mask, axis=0, keepdims=True)
    cnt_ref[...] = carry_ref[...]


def _router(h, w_hi, w_lo, bias, *, tm=512):
    t, d = h.shape
    n_exp = w_hi.shape[1]
    tm = min(tm, t)
    return pl.pallas_call(
        functools.partial(_router_kernel, tm=tm, n_exp=n_exp),
        grid=(t // tm,),
        in_specs=[pl.BlockSpec((tm, d), lambda i: (i, 0)),
                  pl.BlockSpec((d, n_exp), lambda i: (0, 0)),
                  pl.BlockSpec((d, n_exp), lambda i: (0, 0)),
                  pl.BlockSpec((1, n_exp), lambda i: (0, 0))],
        out_specs=[pl.BlockSpec((tm, TOP_K), lambda i: (i, 0)),
                   pl.BlockSpec((tm, TOP_K), lambda i: (i, 0)),
                   pl.BlockSpec((tm, n_exp), lambda i: (i, 0)),
                   pl.BlockSpec((1, n_exp), lambda i: (0, 0))],
        out_shape=[jax.ShapeDtypeStruct((t, TOP_K), I32),
                   jax.ShapeDtypeStruct((t, TOP_K), F32),
                   jax.ShapeDtypeStruct((t, n_exp), F32),
                   jax.ShapeDtypeStruct((1, n_exp), F32)],
        scratch_shapes=[pltpu.VMEM((1, n_exp), F32)],
        compiler_params=_params(1),
        name="router",
    )(h, w_hi, w_lo, bias.reshape(1, n_exp))


def _pos_kernel(idx_ref, rank_ref, off_ref, pos_ref, *, tm, n_exp):
    base = rank_ref[...] + off_ref[...]
    idx = idx_ref[...]
    lane = lax.broadcasted_iota(I32, (tm, n_exp), 1)
    lane_k = lax.broadcasted_iota(I32, (tm, TOP_K), 1)
    pos = jnp.zeros((tm, TOP_K), F32)
    for k in range(TOP_K):
        pk = jnp.sum(jnp.where(lane == idx[:, k:k + 1], base, 0.0), axis=-1, keepdims=True)
        pos = jnp.where(lane_k == k, pk, pos)
    pos_ref[...] = pos.astype(I32)


def _positions(idx, rank, row_off, *, tm=512):
    t, n_exp = rank.shape
    tm = min(tm, t)
    return pl.pallas_call(
        functools.partial(_pos_kernel, tm=tm, n_exp=n_exp),
        grid=(t // tm,),
        in_specs=[pl.BlockSpec((tm, TOP_K), lambda i: (i, 0)),
                  pl.BlockSpec((tm, n_exp), lambda i: (i, 0)),
                  pl.BlockSpec((1, n_exp), lambda i: (0, 0))],
        out_specs=pl.BlockSpec((tm, TOP_K), lambda i: (i, 0)),
        out_shape=jax.ShapeDtypeStruct((t, TOP_K), I32),
        compiler_params=_params(1),
        name="positions",
    )(idx, rank, row_off)


def _dispatch_shared_kernel(pad_lo_ref, pad_hi_ref, pos_ref, hp_ref, h_ref, wg_ref, wu_ref, wd_ref,
                            xs_ref, ys_ref, zero_ref, sem, *, td, n_exp):
    i = pl.program_id(0)

    def row_copy(src_ref, src_row, dst_row, s):
        return pltpu.make_async_copy(src_ref.at[pl.ds(src_row, 1)], xs_ref.at[pl.ds(dst_row, 1)], sem.at[s])

    def issue(n, carry):
        for k in range(TOP_K):
            row_copy(hp_ref, n, pos_ref[n * TOP_K + k], 0).start()
        return carry

    def drain(n, carry):
        for k in range(TOP_K):
            row_copy(hp_ref, 0, 0, 0).wait()
        return carry

    lax.fori_loop(0, td, issue, 0)

    @pl.when(i == 0)
    def _():
        zero_ref[...] = jnp.zeros_like(zero_ref)

        def per_expert(e, carry):
            lo, hi = pad_lo_ref[e], pad_hi_ref[e]

            def zissue(r, c):
                row_copy(zero_ref, 0, r, 1).start()
                return c

            def zdrain(r, c):
                row_copy(zero_ref, 0, 0, 1).wait()
                return c

            lax.fori_loop(lo, hi, zissue, 0)
            lax.fori_loop(lo, hi, zdrain, 0)
            return carry

        lax.fori_loop(0, n_exp, per_expert, 0)

    h = h_ref[...]
    g = _dot(h, wg_ref[...])
    u = _dot(h, wu_ref[...])
    hm = (g * _sigmoid(g) * u).astype(BF16)
    ys_ref[...] = _dot(hm, wd_ref[...]).astype(ys_ref.dtype)

    lax.fori_loop(0, td, drain, 0)


def _dispatch_shared(pos_flat, h_packed, h, wg, wu, wd, pad_lo, pad_hi, n_rows, *, td=256):
    t, dh = h_packed.shape
    d, f = wg.shape
    td = min(td, t)
    n_exp = pad_lo.shape[0]
    grid_spec = pltpu.PrefetchScalarGridSpec(
        num_scalar_prefetch=2,
        grid=(t // td,),
        in_specs=[pl.BlockSpec((td * TOP_K,), lambda i, lo, hi: (i,), memory_space=pltpu.SMEM),
                  pl.BlockSpec((td, dh), lambda i, lo, hi: (i, 0)),
                  pl.BlockSpec((td, d), lambda i, lo, hi: (i, 0)),
                  pl.BlockSpec((d, f), lambda i, lo, hi: (0, 0)),
                  pl.BlockSpec((d, f), lambda i, lo, hi: (0, 0)),
                  pl.BlockSpec((f, d), lambda i, lo, hi: (0, 0))],
        out_specs=[pl.BlockSpec(memory_space=pl.ANY),
                   pl.BlockSpec((td, d), lambda i, lo, hi: (i, 0))],
        scratch_shapes=[pltpu.VMEM((8, dh), h_packed.dtype), pltpu.SemaphoreType.DMA((2,))],
    )
    return pl.pallas_call(
        functools.partial(_dispatch_shared_kernel, td=td, n_exp=n_exp),
        grid_spec=grid_spec,
        out_shape=[jax.ShapeDtypeStruct((n_rows, dh), h_packed.dtype),
                   jax.ShapeDtypeStruct((t, d), BF16)],
        compiler_params=pltpu.CompilerParams(dimension_semantics=("arbitrary",),
                                             vmem_limit_bytes=VMEM_LIMIT_BYTES,
                                             has_side_effects=True),
        name="dispatch_shared",
    )(pad_lo, pad_hi, pos_flat, h_packed, h, wg, wu, wd)


def _expert_kernel(te_ref, ts_in_ref, ts_out_ref, first_ref, nxt_ref, slot_ref, slot_prev_ref, nused_ref,
                   xs_ref, wg_hbm, wu_hbm, wd_hbm, ye_ref,
                   wg_f, wu_f, wd_f, wgu_b, wd_b, hm_even, hm_odd, gu_acc, sem, *, layer, f):
    j = pl.program_id(0)
    n_used = nused_ref[0]
    dh = xs_ref.shape[1]
    cw = dh // EXPERT_CHUNKS

    def weight_copies(e):
        return (pltpu.make_async_copy(wg_hbm.at[layer, e], wg_f, sem.at[0]),
                pltpu.make_async_copy(wu_hbm.at[layer, e], wu_f, sem.at[1]),
                pltpu.make_async_copy(wd_hbm.at[layer, e], wd_f, sem.at[2]))

    @pl.when(j == 0)
    def _():
        for cp in weight_copies(te_ref[0]):
            cp.start()

    @pl.when(first_ref[j] == 1)
    def _():
        for cp in weight_copies(te_ref[j]):
            cp.wait()
        wgu_b[:, :f] = wg_f[...].astype(BF16)
        wgu_b[:, f:] = wu_f[...].astype(BF16)
        wd_b[slot_ref[j]] = wd_f[...].astype(BF16)

        @pl.when(nxt_ref[j] >= 0)
        def _():
            for cp in weight_copies(nxt_ref[j]):
                cp.start()

    def hidden():
        for c in range(EXPERT_CHUNKS):
            lo, hi = _unpack_bf16_pair(xs_ref[:, c * cw:(c + 1) * cw])
            part = (_dot(lo.astype(BF16), wgu_b[c * cw:(c + 1) * cw, :])
                    + _dot(hi.astype(BF16), wgu_b[dh + c * cw:dh + (c + 1) * cw, :]))
            if c == 0:
                gu_acc[...] = part
            else:
                gu_acc[...] += part
        g, u = gu_acc[:, :f], gu_acc[:, f:]
        return (g * _sigmoid(g) * u).astype(BF16)

    def down(hm):
        wd = wd_b.at[slot_prev_ref[j]]
        for c in range(EXPERT_CHUNKS):
            ye_ref[:, c * cw:(c + 1) * cw] = _pack_bf16_pair(
                _dot(hm, wd[:, c * cw:(c + 1) * cw]),
                _dot(hm, wd[:, dh + c * cw:dh + (c + 1) * cw]))

    @pl.when(j == 0)
    def _():
        hm_even[...] = hidden()

    for parity, (hm_cur, hm_prev) in enumerate(((hm_even, hm_odd), (hm_odd, hm_even))):
        @pl.when((j >= 1) & (j < n_used) & (j % 2 == parity))
        def _(hm_cur=hm_cur, hm_prev=hm_prev):
            hm_cur[...] = hidden()
            down(hm_prev[...])

        @pl.when((j >= 1) & (j == n_used) & (j % 2 == parity))
        def _(hm_prev=hm_prev):
            down(hm_prev[...])


def _experts(tile_expert, tile_in, tile_out, tile_first, tile_next, tile_slot, tile_slot_prev, n_used,
             xs, w_gate, w_up, w_down, *, layer, tm):
    n_rows, dh = xs.shape
    d, f = w_gate.shape[2], w_gate.shape[3]
    n_steps = tile_expert.shape[0]
    grid_spec = pltpu.PrefetchScalarGridSpec(
        num_scalar_prefetch=8,
        grid=(n_steps,),
        in_specs=[pl.BlockSpec((tm, dh), lambda j, te, ti, to, fi, nx, sl, sp, nu: (ti[j], 0)),
                  pl.BlockSpec(memory_space=pl.ANY),
                  pl.BlockSpec(memory_space=pl.ANY),
                  pl.BlockSpec(memory_space=pl.ANY)],
        out_specs=pl.BlockSpec((tm, dh), lambda j, te, ti, to, fi, nx, sl, sp, nu: (to[j], 0)),
        scratch_shapes=[pltpu.VMEM((d, f), F32), pltpu.VMEM((d, f), F32), pltpu.VMEM((f, d), F32),
                        pltpu.VMEM((d, 2 * f), BF16), pltpu.VMEM((2, f, d), BF16),
                        pltpu.VMEM((tm, f), BF16), pltpu.VMEM((tm, f), BF16),
                        pltpu.VMEM((tm, 2 * f), F32),
                        pltpu.SemaphoreType.DMA((3,))],
    )
    return pl.pallas_call(
        functools.partial(_expert_kernel, layer=layer, f=f),
        grid_spec=grid_spec,
        out_shape=jax.ShapeDtypeStruct((n_rows, dh), xs.dtype),
        compiler_params=_params(1),
        name="experts",
    )(tile_expert, tile_in, tile_out, tile_first, tile_next, tile_slot, tile_slot_prev, n_used,
      xs, w_gate, w_up, w_down)


COMBINE_GROUP = 16


def _combine_ln_kernel(pos_ref, pos_next_ref, wts_ref, ys_ref, h_ref, g_ref, b_ref, ye_hbm, o_ref,
                       buf0, buf1, sem, *, tc, n_tiles):
    i = pl.program_id(0)
    dh = buf0.shape[2]
    d = 2 * dh

    def issue_group(p_ref, buf, s, r0):
        for dn in range(COMBINE_GROUP):
            for k in range(TOP_K):
                pltpu.make_async_copy(ye_hbm.at[pl.ds(p_ref[(r0 + dn) * TOP_K + k], 1)],
                                      buf.at[k, pl.ds(r0 + dn, 1)], sem.at[s]).start()

    def wait_tile(buf, s):
        def body(n, carry):
            for k in range(TOP_K):
                pltpu.make_async_copy(ye_hbm.at[pl.ds(0, 1)], buf.at[k, pl.ds(n, 1)], sem.at[s]).wait()
            return carry
        lax.fori_loop(0, tc, body, 0)

    def compute_group(buf, r0):
        rows = pl.ds(r0, COMBINE_GROUP)
        w = wts_ref[rows, :]
        y_lo = ys_ref[rows, :dh].astype(F32)
        y_hi = ys_ref[rows, dh:].astype(F32)
        for k in range(TOP_K):
            lo, hi = _unpack_bf16_pair(buf[k, rows, :])
            y_lo = y_lo + w[:, k:k + 1] * lo
            y_hi = y_hi + w[:, k:k + 1] * hi
        z_lo = DEEPNORM_ALPHA * h_ref[rows, :dh].astype(F32) + y_lo
        z_hi = DEEPNORM_ALPHA * h_ref[rows, dh:].astype(F32) + y_hi
        mu = (jnp.sum(z_lo, axis=-1, keepdims=True) + jnp.sum(z_hi, axis=-1, keepdims=True)) * (1.0 / d)
        c_lo, c_hi = z_lo - mu, z_hi - mu
        var = (jnp.sum(c_lo * c_lo, axis=-1, keepdims=True)
               + jnp.sum(c_hi * c_hi, axis=-1, keepdims=True)) * (1.0 / d)
        rstd = lax.rsqrt(var + LN_EPS)
        o_ref[rows, :dh] = (c_lo * rstd * g_ref[:, :dh] + b_ref[:, :dh]).astype(o_ref.dtype)
        o_ref[rows, dh:] = (c_hi * rstd * g_ref[:, dh:] + b_ref[:, dh:]).astype(o_ref.dtype)

    n_groups = tc // COMBINE_GROUP

    @pl.when(i == 0)
    def _():
        def first(gi, carry):
            issue_group(pos_ref, buf0, 0, gi * COMBINE_GROUP)
            return carry
        lax.fori_loop(0, n_groups, first, 0)

    def run(cur, cur_s, nxt, nxt_s):
        wait_tile(cur, cur_s)

        def group(gi, carry):
            r0 = pl.multiple_of(gi * COMBINE_GROUP, COMBINE_GROUP)
            issue_group(pos_next_ref, nxt, nxt_s, r0)
            compute_group(cur, r0)
            return carry
        lax.fori_loop(0, n_groups, group, 0)

        @pl.when(i == n_tiles - 1)
        def _():
            wait_tile(nxt, nxt_s)

    @pl.when(i % 2 == 0)
    def _():
        run(buf0, 0, buf1, 1)

    @pl.when(i % 2 == 1)
    def _():
        run(buf1, 1, buf0, 0)


def _combine_ln(pos_flat, wts, ys, h, ye, ln_g, ln_b, *, tc=128):
    t, d = h.shape
    tc = min(tc, t)
    n_tiles = t // tc
    return pl.pallas_call(
        functools.partial(_combine_ln_kernel, tc=tc, n_tiles=n_tiles),
        grid=(n_tiles,),
        in_specs=[pl.BlockSpec((tc * TOP_K,), lambda i: (i,), memory_space=pltpu.SMEM),
                  pl.BlockSpec((tc * TOP_K,), lambda i: (jnp.minimum(i + 1, n_tiles - 1),),
                               memory_space=pltpu.SMEM),
                  pl.BlockSpec((tc, TOP_K), lambda i: (i, 0)),
                  pl.BlockSpec((tc, d), lambda i: (i, 0)),
                  pl.BlockSpec((tc, d), lambda i: (i, 0)),
                  pl.BlockSpec((1, d), lambda i: (0, 0)),
                  pl.BlockSpec((1, d), lambda i: (0, 0)),
                  pl.BlockSpec(memory_space=pl.ANY)],
        out_specs=pl.BlockSpec((tc, d), lambda i: (i, 0)),
        out_shape=jax.ShapeDtypeStruct((t, d), BF16),
        scratch_shapes=[pltpu.VMEM((TOP_K, tc, d // 2), ye.dtype),
                        pltpu.VMEM((TOP_K, tc, d // 2), ye.dtype),
                        pltpu.SemaphoreType.DMA((2,))],
        compiler_params=_params(1),
        name="combine_ln",
    )(pos_flat, pos_flat, wts, ys, h, ln_g.reshape(1, d), ln_b.reshape(1, d), ye)


def _later_matrix(win):
    j = lax.broadcasted_iota(I32, (win, win), 0)
    s = lax.broadcasted_iota(I32, (win, win), 1)
    return jnp.where(j > s, 1.0, 0.0).astype(BF16)


def _attn_kernel(q_ref, k_ref, v_ref, o_ref, *, seq, heads, tq, win0, win):
    inv_sqrt_d = 1.0 / math.sqrt(HEAD_DIM)
    later = {w: _later_matrix(w) for w in {win0, win}}

    def add_window(w, t0, k_start, k_limit, surv, accs):
        ks = pl.multiple_of(k_start, HEAD_DIM)
        lane = lax.broadcasted_iota(I32, (tq, w), 1)
        rowi = lax.broadcasted_iota(I32, (tq, w), 0)
        kpos = k_start + lane
        valid = (kpos < t0 + rowi) & (kpos < k_limit)
        cols = [slice(h * HEAD_DIM, (h + 1) * HEAD_DIM) for h in range(heads)]
        z = [lax.dot_general(q_ref[pl.ds(t0, tq), c], k_ref[pl.ds(ks, w), c],
                             (((1,), (1,)), ((), ())), preferred_element_type=F32) * inv_sqrt_d
             for c in cols]
        softplus = [jnp.maximum(x, 0.0) + jnp.log(1.0 + jnp.exp(-jnp.abs(x))) for x in z]
        log_fail = [jnp.where(valid, -sp, 0.0) for sp in softplus]
        lf_hi = [lf.astype(BF16) for lf in log_fail]
        lf_lo = [(lf - hi.astype(F32)).astype(BF16) for lf, hi in zip(log_fail, lf_hi)]
        between = [_dot(hi, later[w]) + _dot(lo, later[w]) for hi, lo in zip(lf_hi, lf_lo)]
        a = [jnp.where(valid, jnp.exp(x - sp + b + s), 0.0)
             for x, sp, b, s in zip(z, softplus, between, surv)]
        accs = [acc + _dot(p.astype(BF16), v_ref[pl.ds(ks, w), c]) for acc, p, c in zip(accs, a, cols)]
        surv = [s + jnp.sum(lf, axis=-1, keepdims=True) for s, lf in zip(surv, log_fail)]
        return surv, accs

    def any_alive(surv):
        m = surv[0]
        for s in surv[1:]:
            m = jnp.maximum(m, s)
        return (jnp.max(m) >= EXP_ZERO_BELOW).astype(I32)

    def q_tile(qi, carry):
        t0 = pl.multiple_of(qi * tq, tq)
        start0 = jnp.maximum(t0 + tq - win0, 0)
        surv = [jnp.zeros((tq, 1), F32) for _ in range(heads)]
        accs = [jnp.zeros((tq, HEAD_DIM), F32) for _ in range(heads)]
        surv, accs = add_window(win0, t0, start0, seq + win0, surv, accs)

        def cond(state):
            prev_start, alive, _, _ = state
            return (prev_start > 0) & (alive > 0)

        def body(state):
            prev_start, _, surv, accs = state
            start = jnp.maximum(prev_start - win, 0)
            surv, accs = add_window(win, t0, start, prev_start, surv, accs)
            return start, any_alive(surv), surv, accs

        _, _, _, accs = lax.while_loop(cond, body, (start0, any_alive(surv), surv, accs))
        for h in range(heads):
            o_ref[pl.ds(t0, tq), h * HEAD_DIM:(h + 1) * HEAD_DIM] = accs[h].astype(o_ref.dtype)
        return carry

    lax.fori_loop(0, seq // tq, q_tile, 0)


def _attention(q, k, v, batch, seq, *, heads_per_step=4, tq=128, win0=384, win=256):
    t, d = q.shape
    n_heads = d // HEAD_DIM
    hb = min(heads_per_step, n_heads)
    tq, win0, win = min(tq, seq), min(win0, seq), min(win, seq)
    spec = pl.BlockSpec((seq, hb * HEAD_DIM), lambda b, h: (b, h))
    return pl.pallas_call(
        functools.partial(_attn_kernel, seq=seq, heads=hb, tq=tq, win0=win0, win=win),
        grid=(batch, n_heads // hb),
        in_specs=[spec, spec, spec],
        out_specs=spec,
        out_shape=jax.ShapeDtypeStruct((t, d), BF16),
        compiler_params=_params(2),
        name="attention",
    )(q, k, v)


def _moe_sublayer(layer, h_b, h_packed, w_router, bias, w_gate, w_up, w_down, ws_gate, ws_up, ws_down,
                  ln_g, ln_b, *, tile_rows):
    t, d = h_b.shape
    n_exp = w_router.shape[1]
    w_hi = w_router.astype(BF16)
    w_lo = (w_router - w_hi.astype(F32)).astype(BF16)
    idx, wts, rank, cnt = _router(h_b, w_hi, w_lo, bias)

    counts = cnt[0].astype(I32)
    n_tiles_e = (counts + tile_rows - 1) // tile_rows
    tile_end = jnp.cumsum(n_tiles_e)
    tile_start = tile_end - n_tiles_e
    row_off = tile_start * tile_rows
    n_used = tile_end[-1]
    n_tiles = (t * TOP_K) // tile_rows + n_exp
    n_rows = n_tiles * tile_rows

    def expert_of(tile):
        return jnp.sum((tile_end[None, :] <= tile[:, None]).astype(I32), axis=1)

    def lookup(table, e):
        onehot = e[:, None] == jnp.arange(n_exp, dtype=I32)[None, :]
        return jnp.sum(jnp.where(onehot, table[None, :], 0), axis=1)

    step = jnp.arange(n_tiles + 1, dtype=I32)
    tile_in = jnp.minimum(step, n_used - 1)
    tile_out = jnp.minimum(jnp.maximum(step - 1, 0), n_used - 1)
    tile_expert = expert_of(tile_in)
    tile_first = (step == lookup(tile_start, tile_expert)).astype(I32)
    nxt_tile = lookup(tile_end, tile_expert)
    tile_next = jnp.where(nxt_tile < n_used, expert_of(jnp.minimum(nxt_tile, n_used - 1)), -1).astype(I32)
    tile_slot = (jnp.cumsum(tile_first) - 1) % 2
    tile_slot_prev = jnp.concatenate([tile_slot[:1], tile_slot[:-1]])

    pos = _positions(idx, rank, row_off.astype(F32).reshape(1, n_exp))
    pos_flat = pos.reshape(t * TOP_K)
    xs, ys = _dispatch_shared(pos_flat, h_packed, h_b, ws_gate.astype(BF16), ws_up.astype(BF16),
                              ws_down.astype(BF16), (row_off + counts).astype(I32),
                              (row_off + n_tiles_e * tile_rows).astype(I32), n_rows)
    ye = _experts(tile_expert, tile_in, tile_out, tile_first, tile_next, tile_slot.astype(I32),
                  tile_slot_prev.astype(I32), n_used.reshape(1).astype(I32),
                  xs, w_gate, w_up, w_down, layer=layer, tm=tile_rows)
    return _combine_ln(pos_flat, wts, ys, h_b, ye, ln_g, ln_b)


def kernel(x, p, pool_w_in, pool_w_grp, pool_scale, pool_w_out, kv_w_k, kv_w_v, sb_w_q, sb_w_o,
           moe_w_router, moe_bias, moe_w_gate, moe_w_up, moe_w_down, shared_w_gate, shared_w_up,
           shared_w_down, ple_w_proj, ple_w_gate, ple_b_gate, ln_g, ln_b):
    batch, seq, d = x.shape
    t = batch * seq
    xf = x.reshape(t, d)
    pb = p.reshape(p.shape[0], t, p.shape[-1]).astype(BF16)
    bf = lambda w: w.astype(BF16)

    def moe_and_ple(i, h_b, h_packed, last):
        h_b = _moe_sublayer(i, h_b, h_packed, moe_w_router[i], moe_bias[i], moe_w_gate, moe_w_up,
                            moe_w_down, shared_w_gate[i], shared_w_up[i], shared_w_down[i],
                            ln_g[i, 1], ln_b[i, 1], tile_rows=EXPERT_TILE_ROWS)
        return _mm_ln(h_b, bf(ple_w_gate[i]), h_b, ln_g[i, 2], ln_b[i, 2],
                      out_kinds=(F32,) if last else (BF16,),
                      gate=(pb[i], bf(ple_w_proj[i]), ple_b_gate[i]))[0]

    u = _matmul(bf(xf), bf(pool_w_in[0]))
    mixed = _pool_grp(u, bf(pool_w_grp[0]), pool_scale[0], seq)
    h_b, h_packed = _mm_ln(mixed, bf(pool_w_out[0]), xf, ln_g[0, 0], ln_b[0, 0], out_kinds=(BF16, PACKED))
    h_b = moe_and_ple(0, h_b, h_packed, last=False)

    kk = _matmul(h_b, bf(kv_w_k))
    vv = _matmul(h_b, bf(kv_w_v))
    qq = _matmul(h_b, bf(sb_w_q[0]))
    o = _attention(qq, kk, vv, batch, seq)
    h_b, h_packed = _mm_ln(o, bf(sb_w_o[0]), h_b, ln_g[1, 0], ln_b[1, 0], out_kinds=(BF16, PACKED))
    out = moe_and_ple(1, h_b, h_packed, last=True)
    return out.reshape(batch, seq, d)
```

```python
import functools
import math

import jax
import jax.numpy as jnp
from jax import lax
from jax.experimental import pallas as pl
from jax.experimental.pallas import tpu as pltpu

BF16, F32, I32, U32 = jnp.bfloat16, jnp.float32, jnp.int32, jnp.uint32

DEPTH = 2
POOL_WINDOWS = (2, 4, 8, 16)
POOL_HALO = 16
HEAD_DIM = 128
TOP_K = 8
ROUTED_SCALE = 2.5
LN_EPS = 1e-5
DEEPNORM_ALPHA = (2.0 * DEPTH) ** 0.25
EXP_ZERO_BELOW = -110.0
VMEM_LIMIT_BYTES = 56 * 1024 * 1024
EXPERT_TILE_ROWS = 256
EXPERT_CHUNKS = 4
MXU_COLS = 256
LN_ROW_BLOCK = 32
PACKED = "packed"


def _params(n_axes=1):
    return pltpu.CompilerParams(dimension_semantics=("arbitrary",) * n_axes,
                                vmem_limit_bytes=VMEM_LIMIT_BYTES)


def _round_up(x, m):
    return (x + m - 1) // m * m


def _dot(a, b):
    return jnp.dot(a, b, preferred_element_type=F32)


def _sigmoid(x):
    return 1.0 / (1.0 + jnp.exp(-x))


def _pack_bf16_pair(lo, hi):
    lo_bits = lax.bitcast_convert_type(lo.astype(BF16).astype(F32), U32)
    hi_bits = lax.bitcast_convert_type(hi.astype(BF16).astype(F32), U32)
    return hi_bits | (lo_bits >> 16)


def _unpack_bf16_pair(words):
    lo = lax.bitcast_convert_type(words << 16, F32)
    hi = lax.bitcast_convert_type(words & jnp.uint32(0xFFFF0000), F32)
    return lo, hi


def _mm_kernel(x_ref, w_ref, o_ref):
    o_ref[...] = _dot(x_ref[...], w_ref[...].astype(BF16)).astype(o_ref.dtype)


def _matmul(x, w, *, tm=1024, tn=512, out_dtype=BF16):
    m, k = x.shape
    n = w.shape[1]
    tm, tn = min(tm, m), min(tn, n)
    return pl.pallas_call(
        _mm_kernel,
        grid=(m // tm, n // tn),
        in_specs=[pl.BlockSpec((tm, k), lambda i, j: (i, 0)),
                  pl.BlockSpec((k, tn), lambda i, j: (0, j))],
        out_specs=pl.BlockSpec((tm, tn), lambda i, j: (i, j)),
        out_shape=jax.ShapeDtypeStruct((m, n), out_dtype),
        compiler_params=_params(2),
        name="matmul",
    )(x, w)


def _pool_grp_kernel(u_ref, halo_ref, wg_ref, sc_ref, o_ref, *, tm, tiles_per_seq):
    g = pl.program_id(0)
    seq_tile = pl.program_id(1) % tiles_per_seq
    cur = u_ref[...].astype(F32)
    halo = halo_ref[...].astype(F32)
    halo = jnp.where(seq_tile == 0, 0.0, halo)
    ext = jnp.concatenate([halo, cur], axis=0)
    row = lax.broadcasted_iota(I32, (tm, 1), 0)
    pos1 = (seq_tile * tm + row + 1).astype(F32)

    for gi, w in enumerate(POOL_WINDOWS):
        @pl.when(g == gi)
        def _(w=w):
            s = ext
            span = 1
            while span < w:
                s = s + pltpu.roll(s, span, 0)
                span *= 2
            win = s[POOL_HALO:]
            cnt = jnp.minimum(pos1, float(w))
            pooled = win / cnt - cur
            mixed = _dot(pooled.astype(BF16), wg_ref[...]) * sc_ref[...]
            o_ref[...] = mixed.astype(o_ref.dtype)


def _pool_grp(u, w_grp, scale, seq, *, tm=512):
    t, d = u.shape
    ng, dg, _ = w_grp.shape
    tm = min(tm, seq)
    hb = tm // POOL_HALO
    return pl.pallas_call(
        functools.partial(_pool_grp_kernel, tm=tm, tiles_per_seq=seq // tm),
        grid=(ng, t // tm),
        in_specs=[pl.BlockSpec((tm, dg), lambda g, i: (i, g)),
                  pl.BlockSpec((POOL_HALO, dg), lambda g, i: (jnp.maximum(i * hb - 1, 0), g)),
                  pl.BlockSpec((None, dg, dg), lambda g, i: (g, 0, 0)),
                  pl.BlockSpec((1, dg), lambda g, i: (0, g))],
        out_specs=pl.BlockSpec((tm, dg), lambda g, i: (i, g)),
        out_shape=jax.ShapeDtypeStruct((t, d), BF16),
        compiler_params=_params(2),
        name="pool_grp",
    )(u, u, w_grp, scale.reshape(1, d))


def _row_stats_merge(j, tn, z, mean_ref, m2_ref):
    cm = jnp.mean(z, axis=-1, keepdims=True)
    dz = z - cm
    cm2 = jnp.sum(dz * dz, axis=-1, keepdims=True)

    @pl.when(j == 0)
    def _():
        mean_ref[...] = cm
        m2_ref[...] = cm2

    @pl.when(j > 0)
    def _():
        n_a = (j * tn).astype(F32)
        tot = n_a + float(tn)
        delta = cm - mean_ref[...]
        mean_ref[...] = mean_ref[...] + delta * (float(tn) / tot)
        m2_ref[...] = m2_ref[...] + cm2 + delta * delta * (n_a * float(tn) / tot)


def _mm_ln_kernel(*refs, nj, tn, d, gated, out_kinds):
    if gated:
        lhs_ref, w_ref, res_ref, g_ref, b_ref, p_ref, wp_ref, bg_ref = refs[:8]
        rest = refs[8:]
    else:
        lhs_ref, w_ref, res_ref, g_ref, b_ref = refs[:5]
        rest = refs[5:]
    n_out = len(out_kinds)
    outs, (acc_ref, mean_ref, m2_ref) = rest[:n_out], rest[n_out:]
    j = pl.program_id(1)

    y = _dot(lhs_ref[...], w_ref[...])
    if gated:
        y = _sigmoid(y + bg_ref[...]) * _dot(p_ref[...], wp_ref[...])
    z = DEEPNORM_ALPHA * res_ref[...].astype(F32) + y
    acc_ref[j] = z
    _row_stats_merge(j, tn, z, mean_ref, m2_ref)

    @pl.when(j == nj - 1)
    def _():
        tm = acc_ref.shape[1]
        rb = min(LN_ROW_BLOCK, tm)
        half = nj // 2
        for r in range(tm // rb):
            rows = slice(r * rb, (r + 1) * rb)
            mu = mean_ref[rows, :]
            rstd = lax.rsqrt(m2_ref[rows, :] * (1.0 / d) + LN_EPS)

            def normed(jj):
                sl = slice(jj * tn, (jj + 1) * tn)
                return (acc_ref[jj, rows, :] - mu) * rstd * g_ref[:, sl] + b_ref[:, sl]

            for jj in range(half):
                pair = (jj, jj + half)
                hn = [normed(c) for c in pair]
                for kind, o_ref in zip(out_kinds, outs):
                    if kind == PACKED:
                        o_ref[rows, jj * tn:(jj + 1) * tn] = _pack_bf16_pair(hn[0], hn[1])
                    else:
                        for c, v in zip(pair, hn):
                            o_ref[rows, c * tn:(c + 1) * tn] = v.astype(o_ref.dtype)


def _mm_ln(lhs, w, res, ln_g, ln_b, *, out_kinds=(BF16,), gate=None, tm=512, tn=512):
    t, k = lhs.shape
    d = w.shape[1]
    tm, tn = min(tm, t), min(tn, d // 2)
    nj = d // tn
    in_specs = [pl.BlockSpec((tm, k), lambda i, j: (i, 0)),
                pl.BlockSpec((k, tn), lambda i, j: (0, j)),
                pl.BlockSpec((tm, tn), lambda i, j: (i, j)),
                pl.BlockSpec((1, d), lambda i, j: (0, 0)),
                pl.BlockSpec((1, d), lambda i, j: (0, 0))]
    args = [lhs, w, res, ln_g.reshape(1, d), ln_b.reshape(1, d)]
    if gate is not None:
        p, w_proj, b_gate = gate
        pd = p.shape[1]
        in_specs += [pl.BlockSpec((tm, pd), lambda i, j: (i, 0)),
                     pl.BlockSpec((pd, tn), lambda i, j: (0, j)),
                     pl.BlockSpec((1, tn), lambda i, j: (0, j))]
        args += [p, w_proj, b_gate.reshape(1, d)]
    out_shapes = [jax.ShapeDtypeStruct((t, d // 2), U32) if kind == PACKED
                  else jax.ShapeDtypeStruct((t, d), kind) for kind in out_kinds]
    return pl.pallas_call(
        functools.partial(_mm_ln_kernel, nj=nj, tn=tn, d=d, gated=gate is not None,
                          out_kinds=tuple(out_kinds)),
        grid=(t // tm, nj),
        in_specs=in_specs,
        out_specs=[pl.BlockSpec((tm, s.shape[1]), lambda i, j: (i, 0)) for s in out_shapes],
        out_shape=out_shapes,
        scratch_shapes=[pltpu.VMEM((nj, tm, tn), F32),
                        pltpu.VMEM((tm, 1), F32),
                        pltpu.VMEM((tm, 1), F32)],
        compiler_params=_params(2),
        name="mm_ln_gated" if gate is not None else "mm_ln",
    )(*args)


def _router_kernel(h_ref, whi_ref, wlo_ref, bias_ref, idx_ref, wts_ref, rank_ref, cnt_ref,
                   carry_ref, *, tm, n_exp):
    @pl.when(pl.program_id(0) == 0)
    def _():
        carry_ref[...] = jnp.zeros_like(carry_ref)

    h = h_ref[...]
    scores = _sigmoid(_dot(h, whi_ref[...]) + _dot(h, wlo_ref[...]))
    sel = scores + bias_ref[...]
    lane = lax.broadcasted_iota(I32, (tm, n_exp), 1)
    lane_k = lax.broadcasted_iota(I32, (tm, TOP_K), 1)
    mask = jnp.zeros((tm, n_exp), F32)
    idxs = jnp.zeros((tm, TOP_K), I32)
    wsel = jnp.zeros((tm, TOP_K), F32)
    for k in range(TOP_K):
        m = jnp.max(sel, axis=-1, keepdims=True)
        ik = jnp.min(jnp.where(sel == m, lane, n_exp), axis=-1, keepdims=True)
        onehot = lane == ik
        mask = jnp.where(onehot, 1.0, mask)
        sel = jnp.where(onehot, -jnp.inf, sel)
        sk = jnp.sum(jnp.where(onehot, scores, 0.0), axis=-1, keepdims=True)
        idxs = jnp.where(lane_k == k, ik, idxs)
        wsel = jnp.where(lane_k == k, sk, wsel)
    idx_ref[...] = idxs
    wts_ref[...] = wsel / jnp.sum(wsel, axis=-1, keepdims=True) * ROUTED_SCALE

    r = lax.broadcasted_iota(I32, (tm, tm), 0)
    c = lax.broadcasted_iota(I32, (tm, tm), 1)
    lower = jnp.where(c < r, 1.0, 0.0).astype(BF16)
    rank_ref[...] = carry_ref[...] + _dot(lower, mask.astype(BF16))
    carry_ref[...] = carry_ref[...] + jnp.sum(mask, axis=0, keepdims=True)
    cnt_ref[...] = carry_ref[...]


def _router(h, w_hi, w_lo, bias, *, tm=512):
    t, d = h.shape
    n_exp = w_hi.shape[1]
    tm = min(tm, t)
    return pl.pallas_call(
        functools.partial(_router_kernel, tm=tm, n_exp=n_exp),
        grid=(t // tm,),
        in_specs=[pl.BlockSpec((tm, d), lambda i: (i, 0)),
                  pl.BlockSpec((d, n_exp), lambda i: (0, 0)),
                  pl.BlockSpec((d, n_exp), lambda i: (0, 0)),
                  pl.BlockSpec((1, n_exp), lambda i: (0, 0))],
        out_specs=[pl.BlockSpec((tm, TOP_K), lambda i: (i, 0)),
                   pl.BlockSpec((tm, TOP_K), lambda i: (i, 0)),
                   pl.BlockSpec((tm, n_exp), lambda i: (i, 0)),
                   pl.BlockSpec((1, n_exp), lambda i: (0, 0))],
        out_shape=[jax.ShapeDtypeStruct((t, TOP_K), I32),
                   jax.ShapeDtypeStruct((t, TOP_K), F32),
                   jax.ShapeDtypeStruct((t, n_exp), F32),
                   jax.ShapeDtypeStruct((1, n_exp), F32)],
        scratch_shapes=[pltpu.VMEM((1, n_exp), F32)],
        compiler_params=_params(1),
        name="router",
    )(h, w_hi, w_lo, bias.reshape(1, n_exp))


def _pos_kernel(idx_ref, rank_ref, off_ref, pos_ref, *, tm, n_exp):
    base = rank_ref[...] + off_ref[...]
    idx = idx_ref[...]
    lane = lax.broadcasted_iota(I32, (tm, n_exp), 1)
    lane_k = lax.broadcasted_iota(I32, (tm, TOP_K), 1)
    pos = jnp.zeros((tm, TOP_K), F32)
    for k in range(TOP_K):
        pk = jnp.sum(jnp.where(lane == idx[:, k:k + 1], base, 0.0), axis=-1, keepdims=True)
        pos = jnp.where(lane_k == k, pk, pos)
    pos_ref[...] = pos.astype(I32)


def _positions(idx, rank, row_off, *, tm=512):
    t, n_exp = rank.shape
    tm = min(tm, t)
    return pl.pallas_call(
        functools.partial(_pos_kernel, tm=tm, n_exp=n_exp),
        grid=(t // tm,),
        in_specs=[pl.BlockSpec((tm, TOP_K), lambda i: (i, 0)),
                  pl.BlockSpec((tm, n_exp), lambda i: (i, 0)),
                  pl.BlockSpec((1, n_exp), lambda i: (0, 0))],
        out_specs=pl.BlockSpec((tm, TOP_K), lambda i: (i, 0)),
        out_shape=jax.ShapeDtypeStruct((t, TOP_K), I32),
        compiler_params=_params(1),
        name="positions",
    )(idx, rank, row_off)


def _dispatch_shared_kernel(pad_lo_ref, pad_hi_ref, pos_ref, hp_ref, h_ref, wg_ref, wu_ref, wd_ref,
                            xs_ref, ys_ref, zero_ref, sem, *, td, n_exp):
    i = pl.program_id(0)

    def row_copy(src_ref, src_row, dst_row, s):
        return pltpu.make_async_copy(src_ref.at[pl.ds(src_row, 1)], xs_ref.at[pl.ds(dst_row, 1)], sem.at[s])

    def issue(n, carry):
        for k in range(TOP_K):
            row_copy(hp_ref, n, pos_ref[n * TOP_K + k], 0).start()
        return carry

    def drain(n, carry):
        for k in range(TOP_K):
            row_copy(hp_ref, 0, 0, 0).wait()
        return carry

    lax.fori_loop(0, td, issue, 0)

    @pl.when(i == 0)
    def _():
        zero_ref[...] = jnp.zeros_like(zero_ref)

        def per_expert(e, carry):
            lo, hi = pad_lo_ref[e], pad_hi_ref[e]

            def zissue(r, c):
                row_copy(zero_ref, 0, r, 1).start()
                return c

            def zdrain(r, c):
                row_copy(zero_ref, 0, 0, 1).wait()
                return c

            lax.fori_loop(lo, hi, zissue, 0)
            lax.fori_loop(lo, hi, zdrain, 0)
            return carry

        lax.fori_loop(0, n_exp, per_expert, 0)

    h = h_ref[...]
    g = _dot(h, wg_ref[...])
    u = _dot(h, wu_ref[...])
    hm = (g * _sigmoid(g) * u).astype(BF16)
    ys_ref[...] = _dot(hm, wd_ref[...]).astype(ys_ref.dtype)

    lax.fori_loop(0, td, drain, 0)


def _dispatch_shared(pos_flat, h_packed, h, wg, wu, wd, pad_lo, pad_hi, n_rows, *, td=256):
    t, dh = h_packed.shape
    d, f = wg.shape
    td = min(td, t)
    n_exp = pad_lo.shape[0]
    grid_spec = pltpu.PrefetchScalarGridSpec(
        num_scalar_prefetch=2,
        grid=(t // td,),
        in_specs=[pl.BlockSpec((td * TOP_K,), lambda i, lo, hi: (i,), memory_space=pltpu.SMEM),
                  pl.BlockSpec((td, dh), lambda i, lo, hi: (i, 0)),
                  pl.BlockSpec((td, d), lambda i, lo, hi: (i, 0)),
                  pl.BlockSpec((d, f), lambda i, lo, hi: (0, 0)),
                  pl.BlockSpec((d, f), lambda i, lo, hi: (0, 0)),
                  pl.BlockSpec((f, d), lambda i, lo, hi: (0, 0))],
        out_specs=[pl.BlockSpec(memory_space=pl.ANY),
                   pl.BlockSpec((td, d), lambda i, lo, hi: (i, 0))],
        scratch_shapes=[pltpu.VMEM((8, dh), h_packed.dtype), pltpu.SemaphoreType.DMA((2,))],
    )
    return pl.pallas_call(
        functools.partial(_dispatch_shared_kernel, td=td, n_exp=n_exp),
        grid_spec=grid_spec,
        out_shape=[jax.ShapeDtypeStruct((n_rows, dh), h_packed.dtype),
                   jax.ShapeDtypeStruct((t, d), BF16)],
        compiler_params=pltpu.CompilerParams(dimension_semantics=("arbitrary",),
                                             vmem_limit_bytes=VMEM_LIMIT_BYTES,
                                             has_side_effects=True),
        name="dispatch_shared",
    )(pad_lo, pad_hi, pos_flat, h_packed, h, wg, wu, wd)


def _expert_kernel(te_ref, ts_in_ref, ts_out_ref, first_ref, nxt_ref, slot_ref, slot_prev_ref, nused_ref,
                   xs_ref, wg_hbm, wu_hbm, wd_hbm, ye_ref,
                   wg_f, wu_f, wd_f, wgu_b, wd_b, hm_even, hm_odd, gu_acc, sem, *, layer, f):
    j = pl.program_id(0)
    n_used = nused_ref[0]
    dh = xs_ref.shape[1]
    cw = dh // EXPERT_CHUNKS

    def weight_copies(e):
        return (pltpu.make_async_copy(wg_hbm.at[layer, e], wg_f, sem.at[0]),
                pltpu.make_async_copy(wu_hbm.at[layer, e], wu_f, sem.at[1]),
                pltpu.make_async_copy(wd_hbm.at[layer, e], wd_f, sem.at[2]))

    @pl.when(j == 0)
    def _():
        for cp in weight_copies(te_ref[0]):
            cp.start()

    @pl.when(first_ref[j] == 1)
    def _():
        for cp in weight_copies(te_ref[j]):
            cp.wait()
        wgu_b[:, :f] = wg_f[...].astype(BF16)
        wgu_b[:, f:] = wu_f[...].astype(BF16)
        wd_b[slot_ref[j]] = wd_f[...].astype(BF16)

        @pl.when(nxt_ref[j] >= 0)
        def _():
            for cp in weight_copies(nxt_ref[j]):
                cp.start()

    def hidden():
        for c in range(EXPERT_CHUNKS):
            lo, hi = _unpack_bf16_pair(xs_ref[:, c * cw:(c + 1) * cw])
            part = (_dot(lo.astype(BF16), wgu_b[c * cw:(c + 1) * cw, :])
                    + _dot(hi.astype(BF16), wgu_b[dh + c * cw:dh + (c + 1) * cw, :]))
            if c == 0:
                gu_acc[...] = part
            else:
                gu_acc[...] += part
        g, u = gu_acc[:, :f], gu_acc[:, f:]
        return (g * _sigmoid(g) * u).astype(BF16)

    def down(hm):
        wd = wd_b.at[slot_prev_ref[j]]
        for c in range(EXPERT_CHUNKS):
            ye_ref[:, c * cw:(c + 1) * cw] = _pack_bf16_pair(
                _dot(hm, wd[:, c * cw:(c + 1) * cw]),
                _dot(hm, wd[:, dh + c * cw:dh + (c + 1) * cw]))

    @pl.when(j == 0)
    def _():
        hm_even[...] = hidden()

    for parity, (hm_cur, hm_prev) in enumerate(((hm_even, hm_odd), (hm_odd, hm_even))):
        @pl.when((j >= 1) & (j < n_used) & (j % 2 == parity))
        def _(hm_cur=hm_cur, hm_prev=hm_prev):
            hm_cur[...] = hidden()
            down(hm_prev[...])

        @pl.when((j >= 1) & (j == n_used) & (j % 2 == parity))
        def _(hm_prev=hm_prev):
            down(hm_prev[...])


def _experts(tile_expert, tile_in, tile_out, tile_first, tile_next, tile_slot, tile_slot_prev, n_used,
             xs, w_gate, w_up, w_down, *, layer, tm):
    n_rows, dh = xs.shape
    d, f = w_gate.shape[2], w_gate.shape[3]
    n_steps = tile_expert.shape[0]
    grid_spec = pltpu.PrefetchScalarGridSpec(
        num_scalar_prefetch=8,
        grid=(n_steps,),
        in_specs=[pl.BlockSpec((tm, dh), lambda j, te, ti, to, fi, nx, sl, sp, nu: (ti[j], 0)),
                  pl.BlockSpec(memory_space=pl.ANY),
                  pl.BlockSpec(memory_space=pl.ANY),
                  pl.BlockSpec(memory_space=pl.ANY)],
        out_specs=pl.BlockSpec((tm, dh), lambda j, te, ti, to, fi, nx, sl, sp, nu: (to[j], 0)),
        scratch_shapes=[pltpu.VMEM((d, f), F32), pltpu.VMEM((d, f), F32), pltpu.VMEM((f, d), F32),
                        pltpu.VMEM((d, 2 * f), BF16), pltpu.VMEM((2, f, d), BF16),
                        pltpu.VMEM((tm, f), BF16), pltpu.VMEM((tm, f), BF16),
                        pltpu.VMEM((tm, 2 * f), F32),
                        pltpu.SemaphoreType.DMA((3,))],
    )
    return pl.pallas_call(
        functools.partial(_expert_kernel, layer=layer, f=f),
        grid_spec=grid_spec,
        out_shape=jax.ShapeDtypeStruct((n_rows, dh), xs.dtype),
        compiler_params=_params(1),
        name="experts",
    )(tile_expert, tile_in, tile_out, tile_first, tile_next, tile_slot, tile_slot_prev, n_used,
      xs, w_gate, w_up, w_down)


COMBINE_GROUP = 16


def _combine_ln_kernel(pos_ref, pos_next_ref, wts_ref, ys_ref, h_ref, g_ref, b_ref, ye_hbm, o_ref,
                       buf0, buf1, sem, *, tc, n_tiles):
    i = pl.program_id(0)
    dh = buf0.shape[2]
    d = 2 * dh

    def issue_group(p_ref, buf, s, r0):
        for dn in range(COMBINE_GROUP):
            for k in range(TOP_K):
                pltpu.make_async_copy(ye_hbm.at[pl.ds(p_ref[(r0 + dn) * TOP_K + k], 1)],
                                      buf.at[k, pl.ds(r0 + dn, 1)], sem.at[s]).start()

    def wait_tile(buf, s):
        def body(n, carry):
            for k in range(TOP_K):
                pltpu.make_async_copy(ye_hbm.at[pl.ds(0, 1)], buf.at[k, pl.ds(n, 1)], sem.at[s]).wait()
            return carry
        lax.fori_loop(0, tc, body, 0)

    def compute_group(buf, r0):
        rows = pl.ds(r0, COMBINE_GROUP)
        w = wts_ref[rows, :]
        y_lo = ys_ref[rows, :dh].astype(F32)
        y_hi = ys_ref[rows, dh:].astype(F32)
        for k in range(TOP_K):
            lo, hi = _unpack_bf16_pair(buf[k, rows, :])
            y_lo = y_lo + w[:, k:k + 1] * lo
            y_hi = y_hi + w[:, k:k + 1] * hi
        z_lo = DEEPNORM_ALPHA * h_ref[rows, :dh].astype(F32) + y_lo
        z_hi = DEEPNORM_ALPHA * h_ref[rows, dh:].astype(F32) + y_hi
        mu = (jnp.sum(z_lo, axis=-1, keepdims=True) + jnp.sum(z_hi, axis=-1, keepdims=True)) * (1.0 / d)
        c_lo, c_hi = z_lo - mu, z_hi - mu
        var = (jnp.sum(c_lo * c_lo, axis=-1, keepdims=True)
               + jnp.sum(c_hi * c_hi, axis=-1, keepdims=True)) * (1.0 / d)
        rstd = lax.rsqrt(var + LN_EPS)
        o_ref[rows, :dh] = (c_lo * rstd * g_ref[:, :dh] + b_ref[:, :dh]).astype(o_ref.dtype)
        o_ref[rows, dh:] = (c_hi * rstd * g_ref[:, dh:] + b_ref[:, dh:]).astype(o_ref.dtype)

    n_groups = tc // COMBINE_GROUP

    @pl.when(i == 0)
    def _():
        def first(gi, carry):
            issue_group(pos_ref, buf0, 0, gi * COMBINE_GROUP)
            return carry
        lax.fori_loop(0, n_groups, first, 0)

    def run(cur, cur_s, nxt, nxt_s):
        wait_tile(cur, cur_s)

        def group(gi, carry):
            r0 = pl.multiple_of(gi * COMBINE_GROUP, COMBINE_GROUP)
            issue_group(pos_next_ref, nxt, nxt_s, r0)
            compute_group(cur, r0)
            return carry
        lax.fori_loop(0, n_groups, group, 0)

        @pl.when(i == n_tiles - 1)
        def _():
            wait_tile(nxt, nxt_s)

    @pl.when(i % 2 == 0)
    def _():
        run(buf0, 0, buf1, 1)

    @pl.when(i % 2 == 1)
    def _():
        run(buf1, 1, buf0, 0)


def _combine_ln(pos_flat, wts, ys, h, ye, ln_g, ln_b, *, tc=128):
    t, d = h.shape
    tc = min(tc, t)
    n_tiles = t // tc
    return pl.pallas_call(
        functools.partial(_combine_ln_kernel, tc=tc, n_tiles=n_tiles),
        grid=(n_tiles,),
        in_specs=[pl.BlockSpec((tc * TOP_K,), lambda i: (i,), memory_space=pltpu.SMEM),
                  pl.BlockSpec((tc * TOP_K,), lambda i: (jnp.minimum(i + 1, n_tiles - 1),),
                               memory_space=pltpu.SMEM),
                  pl.BlockSpec((tc, TOP_K), lambda i: (i, 0)),
                  pl.BlockSpec((tc, d), lambda i: (i, 0)),
                  pl.BlockSpec((tc, d), lambda i: (i, 0)),
                  pl.BlockSpec((1, d), lambda i: (0, 0)),
                  pl.BlockSpec((1, d), lambda i: (0, 0)),
                  pl.BlockSpec(memory_space=pl.ANY)],
        out_specs=pl.BlockSpec((tc, d), lambda i: (i, 0)),
        out_shape=jax.ShapeDtypeStruct((t, d), BF16),
        scratch_shapes=[pltpu.VMEM((TOP_K, tc, d // 2), ye.dtype),
                        pltpu.VMEM((TOP_K, tc, d // 2), ye.dtype),
                        pltpu.SemaphoreType.DMA((2,))],
        compiler_params=_params(1),
        name="combine_ln",
    )(pos_flat, pos_flat, wts, ys, h, ln_g.reshape(1, d), ln_b.reshape(1, d), ye)


def _later_matrix(win):
    j = lax.broadcasted_iota(I32, (win, win), 0)
    s = lax.broadcasted_iota(I32, (win, win), 1)
    return jnp.where(j > s, 1.0, 0.0).astype(BF16)


def _attn_kernel(q_ref, k_ref, v_ref, o_ref, *, seq, heads, tq, win0, win):
    inv_sqrt_d = 1.0 / math.sqrt(HEAD_DIM)
    later = {w: _later_matrix(w) for w in {win0, win}}

    def add_window(w, t0, k_start, k_limit, surv, accs):
        ks = pl.multiple_of(k_start, HEAD_DIM)
        lane = lax.broadcasted_iota(I32, (tq, w), 1)
        rowi = lax.broadcasted_iota(I32, (tq, w), 0)
        kpos = k_start + lane
        valid = (kpos < t0 + rowi) & (kpos < k_limit)
        cols = [slice(h * HEAD_DIM, (h + 1) * HEAD_DIM) for h in range(heads)]
        z = [lax.dot_general(q_ref[pl.ds(t0, tq), c], k_ref[pl.ds(ks, w), c],
                             (((1,), (1,)), ((), ())), preferred_element_type=F32) * inv_sqrt_d
             for c in cols]
        softplus = [jnp.maximum(x, 0.0) + jnp.log(1.0 + jnp.exp(-jnp.abs(x))) for x in z]
        log_fail = [jnp.where(valid, -sp, 0.0) for sp in softplus]
        lf_hi = [lf.astype(BF16) for lf in log_fail]
        lf_lo = [(lf - hi.astype(F32)).astype(BF16) for lf, hi in zip(log_fail, lf_hi)]
        between = [_dot(hi, later[w]) + _dot(lo, later[w]) for hi, lo in zip(lf_hi, lf_lo)]
        a = [jnp.where(valid, jnp.exp(x - sp + b + s), 0.0)
             for x, sp, b, s in zip(z, softplus, between, surv)]
        accs = [acc + _dot(p.astype(BF16), v_ref[pl.ds(ks, w), c]) for acc, p, c in zip(accs, a, cols)]
        surv = [s + jnp.sum(lf, axis=-1, keepdims=True) for s, lf in zip(surv, log_fail)]
        return surv, accs

    def any_alive(surv):
        m = surv[0]
        for s in surv[1:]:
            m = jnp.maximum(m, s)
        return (jnp.max(m) >= EXP_ZERO_BELOW).astype(I32)

    def q_tile(qi, carry):
        t0 = pl.multiple_of(qi * tq, tq)
        start0 = jnp.maximum(t0 + tq - win0, 0)
        surv = [jnp.zeros((tq, 1), F32) for _ in range(heads)]
        accs = [jnp.zeros((tq, HEAD_DIM), F32) for _ in range(heads)]
        surv, accs = add_window(win0, t0, start0, seq + win0, surv, accs)

        def cond(state):
            prev_start, alive, _, _ = state
            return (prev_start > 0) & (alive > 0)

        def body(state):
            prev_start, _, surv, accs = state
            start = jnp.maximum(prev_start - win, 0)
            surv, accs = add_window(win, t0, start, prev_start, surv, accs)
            return start, any_alive(surv), surv, accs

        _, _, _, accs = lax.while_loop(cond, body, (start0, any_alive(surv), surv, accs))
        for h in range(heads):
            o_ref[pl.ds(t0, tq), h * HEAD_DIM:(h + 1) * HEAD_DIM] = accs[h].astype(o_ref.dtype)
        return carry

    lax.fori_loop(0, seq // tq, q_tile, 0)


def _attention(q, k, v, batch, seq, *, heads_per_step=4, tq=128, win0=384, win=256):
    t, d = q.shape
    n_heads = d // HEAD_DIM
    hb = min(heads_per_step, n_heads)
    tq, win0, win = min(tq, seq), min(win0, seq), min(win, seq)
    spec = pl.BlockSpec((seq, hb * HEAD_DIM), lambda b, h: (b, h))
    return pl.pallas_call(
        functools.partial(_attn_kernel, seq=seq, heads=hb, tq=tq, win0=win0, win=win),
        grid=(batch, n_heads // hb),
        in_specs=[spec, spec, spec],
        out_specs=spec,
        out_shape=jax.ShapeDtypeStruct((t, d), BF16),
        compiler_params=_params(2),
        name="attention",
    )(q, k, v)


def _moe_sublayer(layer, h_b, h_packed, w_router, bias, w_gate, w_up, w_down, ws_gate, ws_up, ws_down,
                  ln_g, ln_b, *, tile_rows):
    t, d = h_b.shape
    n_exp = w_router.shape[1]
    w_hi = w_router.astype(BF16)
    w_lo = (w_router - w_hi.astype(F32)).astype(BF16)
    idx, wts, rank, cnt = _router(h_b, w_hi, w_lo, bias)

    counts = cnt[0].astype(I32)
    n_tiles_e = (counts + tile_rows - 1) // tile_rows
    tile_end = jnp.cumsum(n_tiles_e)
    tile_start = tile_end - n_tiles_e
    row_off = tile_start * tile_rows
    n_used = tile_end[-1]
    n_tiles = (t * TOP_K) // tile_rows + n_exp
    n_rows = n_tiles * tile_rows

    def expert_of(tile):
        return jnp.sum((tile_end[None, :] <= tile[:, None]).astype(I32), axis=1)

    def lookup(table, e):
        onehot = e[:, None] == jnp.arange(n_exp, dtype=I32)[None, :]
        return jnp.sum(jnp.where(onehot, table[None, :], 0), axis=1)

    step = jnp.arange(n_tiles + 1, dtype=I32)
    tile_in = jnp.minimum(step, n_used - 1)
    tile_out = jnp.minimum(jnp.maximum(step - 1, 0), n_used - 1)
    tile_expert = expert_of(tile_in)
    tile_first = (step == lookup(tile_start, tile_expert)).astype(I32)
    nxt_tile = lookup(tile_end, tile_expert)
    tile_next = jnp.where(nxt_tile < n_used, expert_of(jnp.minimum(nxt_tile, n_used - 1)), -1).astype(I32)
    tile_slot = (jnp.cumsum(tile_first) - 1) % 2
    tile_slot_prev = jnp.concatenate([tile_slot[:1], tile_slot[:-1]])

    pos = _positions(idx, rank, row_off.astype(F32).reshape(1, n_exp))
    pos_flat = pos.reshape(t * TOP_K)
    xs, ys = _dispatch_shared(pos_flat, h_packed, h_b, ws_gate.astype(BF16), ws_up.astype(BF16),
                              ws_down.astype(BF16), (row_off + counts).astype(I32),
                              (row_off + n_tiles_e * tile_rows).astype(I32), n_rows)
    ye = _experts(tile_expert, tile_in, tile_out, tile_first, tile_next, tile_slot.astype(I32),
                  tile_slot_prev.astype(I32), n_used.reshape(1).astype(I32),
                  xs, w_gate, w_up, w_down, layer=layer, tm=tile_rows)
    return _combine_ln(pos_flat, wts, ys, h_b, ye, ln_g, ln_b)


def kernel(x, p, pool_w_in, pool_w_grp, pool_scale, pool_w_out, kv_w_k, kv_w_v, sb_w_q, sb_w_o,
           moe_w_router, moe_bias, moe_w_gate, moe_w_up, moe_w_down, shared_w_gate, shared_w_up,
           shared_w_down, ple_w_proj, ple_w_gate, ple_b_gate, ln_g, ln_b):
    batch, seq, d = x.shape
    t = batch * seq
    xf = x.reshape(t, d)
    pb = p.reshape(p.shape[0], t, p.shape[-1]).astype(BF16)
    bf = lambda w: w.astype(BF16)

    def moe_and_ple(i, h_b, h_packed, last):
        h_b = _moe_sublayer(i, h_b, h_packed, moe_w_router[i], moe_bias[i], moe_w_gate, moe_w_up,
                            moe_w_down, shared_w_gate[i], shared_w_up[i], shared_w_down[i],
                            ln_g[i, 1], ln_b[i, 1], tile_rows=EXPERT_TILE_ROWS)
        return _mm_ln(h_b, bf(ple_w_gate[i]), h_b, ln_g[i, 2], ln_b[i, 2],
                      out_kinds=(F32,) if last else (BF16,),
                      gate=(pb[i], bf(ple_w_proj[i]), ple_b_gate[i]))[0]

    u = _matmul(bf(xf), pool_w_in[0])
    mixed = _pool_grp(u, bf(pool_w_grp[0]), pool_scale[0], seq)
    h_b, h_packed = _mm_ln(mixed, bf(pool_w_out[0]), xf, ln_g[0, 0], ln_b[0, 0], out_kinds=(BF16, PACKED))
    h_b = moe_and_ple(0, h_b, h_packed, last=False)

    kk = _matmul(h_b, kv_w_k)
    vv = _matmul(h_b, kv_w_v)
    qq = _matmul(h_b, sb_w_q[0])
    o = _attention(qq, kk, vv, batch, seq)
    h_b, h_packed = _mm_ln(o, bf(sb_w_o[0]), h_b, ln_g[1, 0], ln_b[1, 0], out_kinds=(BF16, PACKED))
    out = moe_and_ple(1, h_b, h_packed, last=True)
    return out.reshape(batch, seq, d)
```

```python
import functools
import math

import jax
import jax.numpy as jnp
from jax import lax
from jax.experimental import pallas as pl
from jax.experimental.pallas import tpu as pltpu

BF16, F32, I32, U32 = jnp.bfloat16, jnp.float32, jnp.int32, jnp.uint32

DEPTH = 2
POOL_WINDOWS = (2, 4, 8, 16)
POOL_HALO = 16
HEAD_DIM = 128
TOP_K = 8
ROUTED_SCALE = 2.5
LN_EPS = 1e-5
DEEPNORM_ALPHA = (2.0 * DEPTH) ** 0.25
EXP_ZERO_BELOW = -110.0
VMEM_LIMIT_BYTES = 56 * 1024 * 1024
EXPERT_TILE_ROWS = 256
EXPERT_CHUNKS = 4
MXU_COLS = 256
LN_ROW_BLOCK = 32
PACKED = "packed"


def _params(n_axes=1):
    return pltpu.CompilerParams(dimension_semantics=("arbitrary",) * n_axes,
                                vmem_limit_bytes=VMEM_LIMIT_BYTES)


def _round_up(x, m):
    return (x + m - 1) // m * m


def _dot(a, b):
    return jnp.dot(a, b, preferred_element_type=F32)


def _sigmoid(x):
    return 1.0 / (1.0 + jnp.exp(-x))


def _pack_bf16_pair(lo, hi):
    lo_bits = lax.bitcast_convert_type(lo.astype(BF16).astype(F32), U32)
    hi_bits = lax.bitcast_convert_type(hi.astype(BF16).astype(F32), U32)
    return hi_bits | (lo_bits >> 16)


def _unpack_bf16_pair(words):
    lo = lax.bitcast_convert_type(words << 16, F32)
    hi = lax.bitcast_convert_type(words & jnp.uint32(0xFFFF0000), F32)
    return lo, hi


def _mm_kernel(x_ref, w_ref, o_ref):
    o_ref[...] = _dot(x_ref[...], w_ref[...].astype(BF16)).astype(o_ref.dtype)


def _matmul(x, w, *, tm=1024, tn=512, out_dtype=BF16):
    m, k = x.shape
    n = w.shape[1]
    tm, tn = min(tm, m), min(tn, n)
    return pl.pallas_call(
        _mm_kernel,
        grid=(m // tm, n // tn),
        in_specs=[pl.BlockSpec((tm, k), lambda i, j: (i, 0)),
                  pl.BlockSpec((k, tn), lambda i, j: (0, j))],
        out_specs=pl.BlockSpec((tm, tn), lambda i, j: (i, j)),
        out_shape=jax.ShapeDtypeStruct((m, n), out_dtype),
        compiler_params=_params(2),
        name="matmul",
    )(x, w)


def _pool_grp_kernel(u_ref, halo_ref, wg_ref, sc_ref, o_ref, *, tm, tiles_per_seq):
    g = pl.program_id(0)
    seq_tile = pl.program_id(1) % tiles_per_seq
    cur = u_ref[...].astype(F32)
    halo = halo_ref[...].astype(F32)
    halo = jnp.where(seq_tile == 0, 0.0, halo)
    ext = jnp.concatenate([halo, cur], axis=0)
    row = lax.broadcasted_iota(I32, (tm, 1), 0)
    pos1 = (seq_tile * tm + row + 1).astype(F32)

    for gi, w in enumerate(POOL_WINDOWS):
        @pl.when(g == gi)
        def _(w=w):
            s = ext
            span = 1
            while span < w:
                s = s + pltpu.roll(s, span, 0)
                span *= 2
            win = s[POOL_HALO:]
            cnt = jnp.minimum(pos1, float(w))
            pooled = win / cnt - cur
            mixed = _dot(pooled.astype(BF16), wg_ref[...]) * sc_ref[...]
            o_ref[...] = mixed.astype(o_ref.dtype)


def _pool_grp(u, w_grp, scale, seq, *, tm=512):
    t, d = u.shape
    ng, dg, _ = w_grp.shape
    tm = min(tm, seq)
    hb = tm // POOL_HALO
    return pl.pallas_call(
        functools.partial(_pool_grp_kernel, tm=tm, tiles_per_seq=seq // tm),
        grid=(ng, t // tm),
        in_specs=[pl.BlockSpec((tm, dg), lambda g, i: (i, g)),
                  pl.BlockSpec((POOL_HALO, dg), lambda g, i: (jnp.maximum(i * hb - 1, 0), g)),
                  pl.BlockSpec((None, dg, dg), lambda g, i: (g, 0, 0)),
                  pl.BlockSpec((1, dg), lambda g, i: (0, g))],
        out_specs=pl.BlockSpec((tm, dg), lambda g, i: (i, g)),
        out_shape=jax.ShapeDtypeStruct((t, d), BF16),
        compiler_params=_params(2),
        name="pool_grp",
    )(u, u, w_grp, scale.reshape(1, d))


def _row_stats_merge(j, tn, z, mean_ref, m2_ref):
    cm = jnp.mean(z, axis=-1, keepdims=True)
    dz = z - cm
    cm2 = jnp.sum(dz * dz, axis=-1, keepdims=True)

    @pl.when(j == 0)
    def _():
        mean_ref[...] = cm
        m2_ref[...] = cm2

    @pl.when(j > 0)
    def _():
        n_a = (j * tn).astype(F32)
        tot = n_a + float(tn)
        delta = cm - mean_ref[...]
        mean_ref[...] = mean_ref[...] + delta * (float(tn) / tot)
        m2_ref[...] = m2_ref[...] + cm2 + delta * delta * (n_a * float(tn) / tot)


def _mm_ln_kernel(*refs, nj, tn, d, gated, out_kinds):
    if gated:
        lhs_ref, w_ref, res_ref, g_ref, b_ref, p_ref, wp_ref, bg_ref = refs[:8]
        rest = refs[8:]
    else:
        lhs_ref, w_ref, res_ref, g_ref, b_ref = refs[:5]
        rest = refs[5:]
    n_out = len(out_kinds)
    outs, (acc_ref, mean_ref, m2_ref) = rest[:n_out], rest[n_out:]
    j = pl.program_id(1)

    y = _dot(lhs_ref[...], w_ref[...])
    if gated:
        y = _sigmoid(y + bg_ref[...]) * _dot(p_ref[...], wp_ref[...])
    z = DEEPNORM_ALPHA * res_ref[...].astype(F32) + y
    acc_ref[j] = z
    _row_stats_merge(j, tn, z, mean_ref, m2_ref)

    @pl.when(j == nj - 1)
    def _():
        tm = acc_ref.shape[1]
        rb = min(LN_ROW_BLOCK, tm)
        half = nj // 2
        for r in range(tm // rb):
            rows = slice(r * rb, (r + 1) * rb)
            mu = mean_ref[rows, :]
            rstd = lax.rsqrt(m2_ref[rows, :] * (1.0 / d) + LN_EPS)

            def normed(jj):
                sl = slice(jj * tn, (jj + 1) * tn)
                return (acc_ref[jj, rows, :] - mu) * rstd * g_ref[:, sl] + b_ref[:, sl]

            for jj in range(half):
                pair = (jj, jj + half)
                hn = [normed(c) for c in pair]
                for kind, o_ref in zip(out_kinds, outs):
                    if kind == PACKED:
                        o_ref[rows, jj * tn:(jj + 1) * tn] = _pack_bf16_pair(hn[0], hn[1])
                    else:
                        for c, v in zip(pair, hn):
                            o_ref[rows, c * tn:(c + 1) * tn] = v.astype(o_ref.dtype)


def _mm_ln(lhs, w, res, ln_g, ln_b, *, out_kinds=(BF16,), gate=None, tm=512, tn=512):
    t, k = lhs.shape
    d = w.shape[1]
    tm, tn = min(tm, t), min(tn, d // 2)
    nj = d // tn
    in_specs = [pl.BlockSpec((tm, k), lambda i, j: (i, 0)),
                pl.BlockSpec((k, tn), lambda i, j: (0, j)),
                pl.BlockSpec((tm, tn), lambda i, j: (i, j)),
                pl.BlockSpec((1, d), lambda i, j: (0, 0)),
                pl.BlockSpec((1, d), lambda i, j: (0, 0))]
    args = [lhs, w, res, ln_g.reshape(1, d), ln_b.reshape(1, d)]
    if gate is not None:
        p, w_proj, b_gate = gate
        pd = p.shape[1]
        in_specs += [pl.BlockSpec((tm, pd), lambda i, j: (i, 0)),
                     pl.BlockSpec((pd, tn), lambda i, j: (0, j)),
                     pl.BlockSpec((1, tn), lambda i, j: (0, j))]
        args += [p, w_proj, b_gate.reshape(1, d)]
    out_shapes = [jax.ShapeDtypeStruct((t, d // 2), U32) if kind == PACKED
                  else jax.ShapeDtypeStruct((t, d), kind) for kind in out_kinds]
    return pl.pallas_call(
        functools.partial(_mm_ln_kernel, nj=nj, tn=tn, d=d, gated=gate is not None,
                          out_kinds=tuple(out_kinds)),
        grid=(t // tm, nj),
        in_specs=in_specs,
        out_specs=[pl.BlockSpec((tm, s.shape[1]), lambda i, j: (i, 0)) for s in out_shapes],
        out_shape=out_shapes,
        scratch_shapes=[pltpu.VMEM((nj, tm, tn), F32),
                        pltpu.VMEM((tm, 1), F32),
                        pltpu.VMEM((tm, 1), F32)],
        compiler_params=_params(2),
        name="mm_ln_gated" if gate is not None else "mm_ln",
    )(*args)


def _router_kernel(h_ref, whi_ref, wlo_ref, bias_ref, idx_ref, wts_ref, rank_ref, cnt_ref,
                   carry_ref, *, tm, n_exp):
    @pl.when(pl.program_id(0) == 0)
    def _():
        carry_ref[...] = jnp.zeros_like(carry_ref)

    h = h_ref[...]
    scores = _sigmoid(_dot(h, whi_ref[...]) + _dot(h, wlo_ref[...]))
    sel = scores + bias_ref[...]
    lane = lax.broadcasted_iota(I32, (tm, n_exp), 1)
    lane_k = lax.broadcasted_iota(I32, (tm, TOP_K), 1)
    mask = jnp.zeros((tm, n_exp), F32)
    idxs = jnp.zeros((tm, TOP_K), I32)
    wsel = jnp.zeros((tm, TOP_K), F32)
    for k in range(TOP_K):
        m = jnp.max(sel, axis=-1, keepdims=True)
        ik = jnp.min(jnp.where(sel == m, lane, n_exp), axis=-1, keepdims=True)
        onehot = lane == ik
        mask = jnp.where(onehot, 1.0, mask)
        sel = jnp.where(onehot, -jnp.inf, sel)
        sk = jnp.sum(jnp.where(onehot, scores, 0.0), axis=-1, keepdims=True)
        idxs = jnp.where(lane_k == k, ik, idxs)
        wsel = jnp.where(lane_k == k, sk, wsel)
    idx_ref[...] = idxs
    wts_ref[...] = wsel / jnp.sum(wsel, axis=-1, keepdims=True) * ROUTED_SCALE

    r = lax.broadcasted_iota(I32, (tm, tm), 0)
    c = lax.broadcasted_iota(I32, (tm, tm), 1)
    lower = jnp.where(c < r, 1.0, 0.0).astype(BF16)
    rank_ref[...] = carry_ref[...] + _dot(lower, mask.astype(BF16))
    carry_ref[...] = carry_ref[...] + jnp.sum(mask, axis=0, keepdims=True)
    cnt_ref[...] = carry_ref[...]


def _router(h, w_hi, w_lo, bias, *, tm=512):
    t, d = h.shape
    n_exp = w_hi.shape[1]
    tm = min(tm, t)
    return pl.pallas_call(
        functools.partial(_router_kernel, tm=tm, n_exp=n_exp),
        grid=(t // tm,),
        in_specs=[pl.BlockSpec((tm, d), lambda i: (i, 0)),
                  pl.BlockSpec((d, n_exp), lambda i: (0, 0)),
                  pl.BlockSpec((d, n_exp), lambda i: (0, 0)),
                  pl.BlockSpec((1, n_exp), lambda i: (0, 0))],
        out_specs=[pl.BlockSpec((tm, TOP_K), lambda i: (i, 0)),
                   pl.BlockSpec((tm, TOP_K), lambda i: (i, 0)),
                   pl.BlockSpec((tm, n_exp), lambda i: (i, 0)),
                   pl.BlockSpec((1, n_exp), lambda i: (0, 0))],
        out_shape=[jax.ShapeDtypeStruct((t, TOP_K), I32),
                   jax.ShapeDtypeStruct((t, TOP_K), F32),
                   jax.ShapeDtypeStruct((t, n_exp), F32),
                   jax.ShapeDtypeStruct((1, n_exp), F32)],
        scratch_shapes=[pltpu.VMEM((1, n_exp), F32)],
        compiler_params=_params(1),
        name="router",
    )(h, w_hi, w_lo, bias.reshape(1, n_exp))


def _pos_kernel(idx_ref, rank_ref, off_ref, pos_ref, *, tm, n_exp):
    base = rank_ref[...] + off_ref[...]
    idx = idx_ref[...]
    lane = lax.broadcasted_iota(I32, (tm, n_exp), 1)
    lane_k = lax.broadcasted_iota(I32, (tm, TOP_K), 1)
    pos = jnp.zeros((tm, TOP_K), F32)
    for k in range(TOP_K):
        pk = jnp.sum(jnp.where(lane == idx[:, k:k + 1], base, 0.0), axis=-1, keepdims=True)
        pos = jnp.where(lane_k == k, pk, pos)
    pos_ref[...] = pos.astype(I32)


def _positions(idx, rank, row_off, *, tm=512):
    t, n_exp = rank.shape
    tm = min(tm, t)
    return pl.pallas_call(
        functools.partial(_pos_kernel, tm=tm, n_exp=n_exp),
        grid=(t // tm,),
        in_specs=[pl.BlockSpec((tm, TOP_K), lambda i: (i, 0)),
                  pl.BlockSpec((tm, n_exp), lambda i: (i, 0)),
                  pl.BlockSpec((1, n_exp), lambda i: (0, 0))],
        out_specs=pl.BlockSpec((tm, TOP_K), lambda i: (i, 0)),
        out_shape=jax.ShapeDtypeStruct((t, TOP_K), I32),
        compiler_params=_params(1),
        name="positions",
    )(idx, rank, row_off)


def _shared_kernel(h_ref, wg_ref, wu_ref, wd_ref, o_ref):
    h = h_ref[...]
    g = _dot(h, wg_ref[...])
    u = _dot(h, wu_ref[...])
    hm = (g * _sigmoid(g) * u).astype(BF16)
    o_ref[...] = _dot(hm, wd_ref[...]).astype(o_ref.dtype)


def _shared_expert(h, wg, wu, wd, *, tm=512):
    t, d = h.shape
    f = wg.shape[1]
    tm = min(tm, t)
    return pl.pallas_call(
        _shared_kernel,
        grid=(t // tm,),
        in_specs=[pl.BlockSpec((tm, d), lambda i: (i, 0)),
                  pl.BlockSpec((d, f), lambda i: (0, 0)),
                  pl.BlockSpec((d, f), lambda i: (0, 0)),
                  pl.BlockSpec((f, d), lambda i: (0, 0))],
        out_specs=pl.BlockSpec((tm, d), lambda i: (i, 0)),
        out_shape=jax.ShapeDtypeStruct((t, d), BF16),
        compiler_params=_params(1),
        name="shared_expert",
    )(h, wg, wu, wd)


def _expert_kernel(te_ref, ts_out_ref, first_ref, nxt_ref, slot_ref, slot_prev_ref, nused_ref,
                   tok_ref, tok_next_ref, hp_hbm, wg_hbm, wu_hbm, wd_hbm, ye_ref,
                   wg_f, wu_f, wd_f, wgu_b, wd_b, hm_even, hm_odd, gu_acc, x_even, x_odd, sem, gsem,
                   *, layer, f):
    j = pl.program_id(0)
    n_used = nused_ref[0]
    tm, dh = x_even.shape
    cw = dh // EXPERT_CHUNKS

    def weight_copies(e):
        return (pltpu.make_async_copy(wg_hbm.at[layer, e], wg_f, sem.at[0]),
                pltpu.make_async_copy(wu_hbm.at[layer, e], wu_f, sem.at[1]),
                pltpu.make_async_copy(wd_hbm.at[layer, e], wd_f, sem.at[2]))

    def gather_row(t_ref, x_buf, s, r):
        pltpu.make_async_copy(hp_hbm.at[pl.ds(t_ref[r], 1)], x_buf.at[pl.ds(r, 1)], gsem.at[s]).start()

    def gather_rows(t_ref, x_buf, s, unrolled=True):
        if unrolled:
            for r in range(tm):
                gather_row(t_ref, x_buf, s, r)
        else:
            def body(r, carry):
                gather_row(t_ref, x_buf, s, r)
                return carry
            lax.fori_loop(0, tm, body, 0)

    def wait_rows(x_buf, s):
        pltpu.make_async_copy(hp_hbm.at[pl.ds(0, tm)], x_buf, gsem.at[s]).wait()

    @pl.when(j == 0)
    def _():
        for cp in weight_copies(te_ref[0]):
            cp.start()

    @pl.when(first_ref[j] == 1)
    def _():
        for cp in weight_copies(te_ref[j]):
            cp.wait()
        wgu_b[:, :f] = wg_f[...].astype(BF16)
        wgu_b[:, f:] = wu_f[...].astype(BF16)
        wd_b[slot_ref[j]] = wd_f[...].astype(BF16)

        @pl.when(nxt_ref[j] >= 0)
        def _():
            for cp in weight_copies(nxt_ref[j]):
                cp.start()

    def hidden(x_buf):
        for c in range(EXPERT_CHUNKS):
            lo, hi = _unpack_bf16_pair(x_buf[:, c * cw:(c + 1) * cw])
            part = (_dot(lo.astype(BF16), wgu_b[c * cw:(c + 1) * cw, :])
                    + _dot(hi.astype(BF16), wgu_b[dh + c * cw:dh + (c + 1) * cw, :]))
            if c == 0:
                gu_acc[...] = part
            else:
                gu_acc[...] += part
        g, u = gu_acc[:, :f], gu_acc[:, f:]
        return (g * _sigmoid(g) * u).astype(BF16)

    def down(hm):
        wd = wd_b.at[slot_prev_ref[j]]
        for c in range(EXPERT_CHUNKS):
            ye_ref[:, c * cw:(c + 1) * cw] = _pack_bf16_pair(
                _dot(hm, wd[:, c * cw:(c + 1) * cw]),
                _dot(hm, wd[:, dh + c * cw:dh + (c + 1) * cw]))

    @pl.when(j == 0)
    def _():
        gather_rows(tok_ref, x_even, 0, unrolled=False)
        wait_rows(x_even, 0)
        gather_rows(tok_next_ref, x_odd, 1, unrolled=False)
        hm_even[...] = hidden(x_even)

    bufs = ((hm_even, x_even), (hm_odd, x_odd))
    for parity in range(2):
        (hm_cur, x_cur), (hm_prev, x_nxt) = bufs[parity], bufs[1 - parity]

        @pl.when((j >= 1) & (j < n_used) & (j % 2 == parity))
        def _(hm_cur=hm_cur, x_cur=x_cur, hm_prev=hm_prev, x_nxt=x_nxt, parity=parity):
            wait_rows(x_cur, parity)
            gather_rows(tok_next_ref, x_nxt, 1 - parity)
            hm_cur[...] = hidden(x_cur)
            down(hm_prev[...])

        @pl.when((j >= 1) & (j == n_used) & (j % 2 == parity))
        def _(x_cur=x_cur, hm_prev=hm_prev, parity=parity):
            wait_rows(x_cur, parity)
            down(hm_prev[...])


def _experts(tile_expert, tile_in, tile_in_next, tile_out, tile_first, tile_next, tile_slot, tile_slot_prev,
             n_used, tok_rows, h_packed, w_gate, w_up, w_down, *, layer, tm):
    t, dh = h_packed.shape
    d, f = w_gate.shape[2], w_gate.shape[3]
    n_steps = tile_expert.shape[0]
    n_rows = (n_steps - 1) * tm
    smem_tile = lambda which: pl.BlockSpec(
        (tm,), lambda j, te, ti, tn, to, fi, nx, sl, sp, nu: ((tn if which else ti)[j],),
        memory_space=pltpu.SMEM)
    grid_spec = pltpu.PrefetchScalarGridSpec(
        num_scalar_prefetch=9,
        grid=(n_steps,),
        in_specs=[smem_tile(0), smem_tile(1),
                  pl.BlockSpec(memory_space=pl.ANY),
                  pl.BlockSpec(memory_space=pl.ANY),
                  pl.BlockSpec(memory_space=pl.ANY),
                  pl.BlockSpec(memory_space=pl.ANY)],
        out_specs=pl.BlockSpec((tm, dh), lambda j, te, ti, tn, to, fi, nx, sl, sp, nu: (to[j], 0)),
        scratch_shapes=[pltpu.VMEM((d, f), F32), pltpu.VMEM((d, f), F32), pltpu.VMEM((f, d), F32),
                        pltpu.VMEM((d, 2 * f), BF16), pltpu.VMEM((2, f, d), BF16),
                        pltpu.VMEM((tm, f), BF16), pltpu.VMEM((tm, f), BF16),
                        pltpu.VMEM((tm, 2 * f), F32),
                        pltpu.VMEM((tm, dh), h_packed.dtype), pltpu.VMEM((tm, dh), h_packed.dtype),
                        pltpu.SemaphoreType.DMA((3,)), pltpu.SemaphoreType.DMA((2,))],
    )
    kern = functools.partial(_expert_kernel, layer=layer, f=f)

    def body(te, ti, tn, to, fi, nx, sl, sp, nu, *rest):
        return kern(te, to, fi, nx, sl, sp, nu, *rest)

    return pl.pallas_call(
        body,
        grid_spec=grid_spec,
        out_shape=jax.ShapeDtypeStruct((n_rows, dh), h_packed.dtype),
        compiler_params=_params(1),
        name="experts",
    )(tile_expert, tile_in, tile_in_next, tile_out, tile_first, tile_next, tile_slot, tile_slot_prev, n_used,
      tok_rows, tok_rows, h_packed, w_gate, w_up, w_down)


COMBINE_GROUP = 16


def _combine_ln_kernel(pos_ref, pos_next_ref, wts_ref, ys_ref, h_ref, g_ref, b_ref, ye_hbm, o_ref,
                       buf0, buf1, sem, *, tc, n_tiles):
    i = pl.program_id(0)
    dh = buf0.shape[2]
    d = 2 * dh

    def issue_group(p_ref, buf, s, r0):
        for dn in range(COMBINE_GROUP):
            for k in range(TOP_K):
                pltpu.make_async_copy(ye_hbm.at[pl.ds(p_ref[(r0 + dn) * TOP_K + k], 1)],
                                      buf.at[k, pl.ds(r0 + dn, 1)], sem.at[s]).start()

    def wait_tile(buf, s):
        def body(n, carry):
            for k in range(TOP_K):
                pltpu.make_async_copy(ye_hbm.at[pl.ds(0, 1)], buf.at[k, pl.ds(n, 1)], sem.at[s]).wait()
            return carry
        lax.fori_loop(0, tc, body, 0)

    def compute_group(buf, r0):
        rows = pl.ds(r0, COMBINE_GROUP)
        w = wts_ref[rows, :]
        y_lo = ys_ref[rows, :dh].astype(F32)
        y_hi = ys_ref[rows, dh:].astype(F32)
        for k in range(TOP_K):
            lo, hi = _unpack_bf16_pair(buf[k, rows, :])
            y_lo = y_lo + w[:, k:k + 1] * lo
            y_hi = y_hi + w[:, k:k + 1] * hi
        z_lo = DEEPNORM_ALPHA * h_ref[rows, :dh].astype(F32) + y_lo
        z_hi = DEEPNORM_ALPHA * h_ref[rows, dh:].astype(F32) + y_hi
        mu = (jnp.sum(z_lo, axis=-1, keepdims=True) + jnp.sum(z_hi, axis=-1, keepdims=True)) * (1.0 / d)
        c_lo, c_hi = z_lo - mu, z_hi - mu
        var = (jnp.sum(c_lo * c_lo, axis=-1, keepdims=True)
               + jnp.sum(c_hi * c_hi, axis=-1, keepdims=True)) * (1.0 / d)
        rstd = lax.rsqrt(var + LN_EPS)
        o_ref[rows, :dh] = (c_lo * rstd * g_ref[:, :dh] + b_ref[:, :dh]).astype(o_ref.dtype)
        o_ref[rows, dh:] = (c_hi * rstd * g_ref[:, dh:] + b_ref[:, dh:]).astype(o_ref.dtype)

    n_groups = tc // COMBINE_GROUP

    @pl.when(i == 0)
    def _():
        def first(gi, carry):
            issue_group(pos_ref, buf0, 0, gi * COMBINE_GROUP)
            return carry
        lax.fori_loop(0, n_groups, first, 0)

    def run(cur, cur_s, nxt, nxt_s):
        wait_tile(cur, cur_s)

        def group(gi, carry):
            r0 = pl.multiple_of(gi * COMBINE_GROUP, COMBINE_GROUP)
            issue_group(pos_next_ref, nxt, nxt_s, r0)
            compute_group(cur, r0)
            return carry
        lax.fori_loop(0, n_groups, group, 0)

        @pl.when(i == n_tiles - 1)
        def _():
            wait_tile(nxt, nxt_s)

    @pl.when(i % 2 == 0)
    def _():
        run(buf0, 0, buf1, 1)

    @pl.when(i % 2 == 1)
    def _():
        run(buf1, 1, buf0, 0)


def _combine_ln(pos_flat, wts, ys, h, ye, ln_g, ln_b, *, tc=128):
    t, d = h.shape
    tc = min(tc, t)
    n_tiles = t // tc
    return pl.pallas_call(
        functools.partial(_combine_ln_kernel, tc=tc, n_tiles=n_tiles),
        grid=(n_tiles,),
        in_specs=[pl.BlockSpec((tc * TOP_K,), lambda i: (i,), memory_space=pltpu.SMEM),
                  pl.BlockSpec((tc * TOP_K,), lambda i: (jnp.minimum(i + 1, n_tiles - 1),),
                               memory_space=pltpu.SMEM),
                  pl.BlockSpec((tc, TOP_K), lambda i: (i, 0)),
                  pl.BlockSpec((tc, d), lambda i: (i, 0)),
                  pl.BlockSpec((tc, d), lambda i: (i, 0)),
                  pl.BlockSpec((1, d), lambda i: (0, 0)),
                  pl.BlockSpec((1, d), lambda i: (0, 0)),
                  pl.BlockSpec(memory_space=pl.ANY)],
        out_specs=pl.BlockSpec((tc, d), lambda i: (i, 0)),
        out_shape=jax.ShapeDtypeStruct((t, d), BF16),
        scratch_shapes=[pltpu.VMEM((TOP_K, tc, d // 2), ye.dtype),
                        pltpu.VMEM((TOP_K, tc, d // 2), ye.dtype),
                        pltpu.SemaphoreType.DMA((2,))],
        compiler_params=_params(1),
        name="combine_ln",
    )(pos_flat, pos_flat, wts, ys, h, ln_g.reshape(1, d), ln_b.reshape(1, d), ye)


def _later_matrix(win):
    j = lax.broadcasted_iota(I32, (win, win), 0)
    s = lax.broadcasted_iota(I32, (win, win), 1)
    return jnp.where(j > s, 1.0, 0.0).astype(BF16)


def _attn_kernel(q_ref, k_ref, v_ref, o_ref, *, seq, heads, tq, win0, win):
    inv_sqrt_d = 1.0 / math.sqrt(HEAD_DIM)
    later = {w: _later_matrix(w) for w in {win0, win}}

    def add_window(w, t0, k_start, k_limit, surv, accs):
        ks = pl.multiple_of(k_start, HEAD_DIM)
        lane = lax.broadcasted_iota(I32, (tq, w), 1)
        rowi = lax.broadcasted_iota(I32, (tq, w), 0)
        kpos = k_start + lane
        valid = (kpos < t0 + rowi) & (kpos < k_limit)
        cols = [slice(h * HEAD_DIM, (h + 1) * HEAD_DIM) for h in range(heads)]
        z = [lax.dot_general(q_ref[pl.ds(t0, tq), c], k_ref[pl.ds(ks, w), c],
                             (((1,), (1,)), ((), ())), preferred_element_type=F32) * inv_sqrt_d
             for c in cols]
        softplus = [jnp.maximum(x, 0.0) + jnp.log(1.0 + jnp.exp(-jnp.abs(x))) for x in z]
        log_fail = [jnp.where(valid, -sp, 0.0) for sp in softplus]
        lf_hi = [lf.astype(BF16) for lf in log_fail]
        lf_lo = [(lf - hi.astype(F32)).astype(BF16) for lf, hi in zip(log_fail, lf_hi)]
        between = [_dot(hi, later[w]) + _dot(lo, later[w]) for hi, lo in zip(lf_hi, lf_lo)]
        a = [jnp.where(valid, jnp.exp(x - sp + b + s), 0.0)
             for x, sp, b, s in zip(z, softplus, between, surv)]
        accs = [acc + _dot(p.astype(BF16), v_ref[pl.ds(ks, w), c]) for acc, p, c in zip(accs, a, cols)]
        surv = [s + jnp.sum(lf, axis=-1, keepdims=True) for s, lf in zip(surv, log_fail)]
        return surv, accs

    def any_alive(surv):
        m = surv[0]
        for s in surv[1:]:
            m = jnp.maximum(m, s)
        return (jnp.max(m) >= EXP_ZERO_BELOW).astype(I32)

    def q_tile(qi, carry):
        t0 = pl.multiple_of(qi * tq, tq)
        start0 = jnp.maximum(t0 + tq - win0, 0)
        surv = [jnp.zeros((tq, 1), F32) for _ in range(heads)]
        accs = [jnp.zeros((tq, HEAD_DIM), F32) for _ in range(heads)]
        surv, accs = add_window(win0, t0, start0, seq + win0, surv, accs)

        def cond(state):
            prev_start, alive, _, _ = state
            return (prev_start > 0) & (alive > 0)

        def body(state):
            prev_start, _, surv, accs = state
            start = jnp.maximum(prev_start - win, 0)
            surv, accs = add_window(win, t0, start, prev_start, surv, accs)
            return start, any_alive(surv), surv, accs

        _, _, _, accs = lax.while_loop(cond, body, (start0, any_alive(surv), surv, accs))
        for h in range(heads):
            o_ref[pl.ds(t0, tq), h * HEAD_DIM:(h + 1) * HEAD_DIM] = accs[h].astype(o_ref.dtype)
        return carry

    lax.fori_loop(0, seq // tq, q_tile, 0)


def _attention(q, k, v, batch, seq, *, heads_per_step=4, tq=128, win0=384, win=256):
    t, d = q.shape
    n_heads = d // HEAD_DIM
    hb = min(heads_per_step, n_heads)
    tq, win0, win = min(tq, seq), min(win0, seq), min(win, seq)
    spec = pl.BlockSpec((seq, hb * HEAD_DIM), lambda b, h: (b, h))
    return pl.pallas_call(
        functools.partial(_attn_kernel, seq=seq, heads=hb, tq=tq, win0=win0, win=win),
        grid=(batch, n_heads // hb),
        in_specs=[spec, spec, spec],
        out_specs=spec,
        out_shape=jax.ShapeDtypeStruct((t, d), BF16),
        compiler_params=_params(2),
        name="attention",
    )(q, k, v)


def _moe_sublayer(layer, h_b, h_packed, w_router, bias, w_gate, w_up, w_down, ws_gate, ws_up, ws_down,
                  ln_g, ln_b, *, tile_rows):
    t, d = h_b.shape
    n_exp = w_router.shape[1]
    w_hi = w_router.astype(BF16)
    w_lo = (w_router - w_hi.astype(F32)).astype(BF16)
    idx, wts, rank, cnt = _router(h_b, w_hi, w_lo, bias)

    counts = cnt[0].astype(I32)
    n_tiles_e = (counts + tile_rows - 1) // tile_rows
    tile_end = jnp.cumsum(n_tiles_e)
    tile_start = tile_end - n_tiles_e
    row_off = tile_start * tile_rows
    n_used = tile_end[-1]
    n_tiles = (t * TOP_K) // tile_rows + n_exp
    n_rows = n_tiles * tile_rows

    def expert_of(tile):
        return jnp.sum((tile_end[None, :] <= tile[:, None]).astype(I32), axis=1)

    def lookup(table, e):
        onehot = e[:, None] == jnp.arange(n_exp, dtype=I32)[None, :]
        return jnp.sum(jnp.where(onehot, table[None, :], 0), axis=1)

    step = jnp.arange(n_tiles + 1, dtype=I32)
    tile_in = jnp.minimum(step, n_used - 1)
    tile_out = jnp.minimum(jnp.maximum(step - 1, 0), n_used - 1)
    tile_expert = expert_of(tile_in)
    tile_first = (step == lookup(tile_start, tile_expert)).astype(I32)
    nxt_tile = lookup(tile_end, tile_expert)
    tile_next = jnp.where(nxt_tile < n_used, expert_of(jnp.minimum(nxt_tile, n_used - 1)), -1).astype(I32)
    tile_slot = (jnp.cumsum(tile_first) - 1) % 2
    tile_slot_prev = jnp.concatenate([tile_slot[:1], tile_slot[:-1]])

    pos = _positions(idx, rank, row_off.astype(F32).reshape(1, n_exp))
    pos_flat = pos.reshape(t * TOP_K)

    tok_sorted = jnp.argsort(pos_flat).astype(I32) // TOP_K
    tok_sorted = jnp.concatenate([tok_sorted, jnp.zeros((tile_rows,), I32)])
    holes_before = row_off - (jnp.cumsum(counts) - counts)
    first_tok = tile_in * tile_rows - lookup(holes_before, tile_expert)
    tok_rows = jax.vmap(lambda s0: lax.dynamic_slice(tok_sorted, (s0,), (tile_rows,)))(first_tok)
    tile_in_next = jnp.concatenate([tile_in[1:], tile_in[-1:]])

    ye = _experts(tile_expert, tile_in, tile_in_next, tile_out, tile_first, tile_next, tile_slot.astype(I32),
                  tile_slot_prev.astype(I32), n_used.reshape(1).astype(I32), tok_rows.reshape(-1),
                  h_packed, w_gate, w_up, w_down, layer=layer, tm=tile_rows)
    ys = _shared_expert(h_b, ws_gate.astype(BF16), ws_up.astype(BF16), ws_down.astype(BF16))
    return _combine_ln(pos_flat, wts, ys, h_b, ye, ln_g, ln_b)


def kernel(x, p, pool_w_in, pool_w_grp, pool_scale, pool_w_out, kv_w_k, kv_w_v, sb_w_q, sb_w_o,
           moe_w_router, moe_bias, moe_w_gate, moe_w_up, moe_w_down, shared_w_gate, shared_w_up,
           shared_w_down, ple_w_proj, ple_w_gate, ple_b_gate, ln_g, ln_b):
    batch, seq, d = x.shape
    t = batch * seq
    xf = x.reshape(t, d)
    pb = p.reshape(p.shape[0], t, p.shape[-1]).astype(BF16)
    bf = lambda w: w.astype(BF16)

    def moe_and_ple(i, h_b, h_packed, last):
        h_b = _moe_sublayer(i, h_b, h_packed, moe_w_router[i], moe_bias[i], moe_w_gate, moe_w_up,
                            moe_w_down, shared_w_gate[i], shared_w_up[i], shared_w_down[i],
                            ln_g[i, 1], ln_b[i, 1], tile_rows=EXPERT_TILE_ROWS)
        return _mm_ln(h_b, bf(ple_w_gate[i]), h_b, ln_g[i, 2], ln_b[i, 2],
                      out_kinds=(F32,) if last else (BF16,),
                      gate=(pb[i], bf(ple_w_proj[i]), ple_b_gate[i]))[0]

    u = _matmul(bf(xf), pool_w_in[0])
    mixed = _pool_grp(u, bf(pool_w_grp[0]), pool_scale[0], seq)
    h_b, h_packed = _mm_ln(mixed, bf(pool_w_out[0]), xf, ln_g[0, 0], ln_b[0, 0], out_kinds=(BF16, PACKED))
    h_b = moe_and_ple(0, h_b, h_packed, last=False)

    kk = _matmul(h_b, kv_w_k)
    vv = _matmul(h_b, kv_w_v)
    qq = _matmul(h_b, sb_w_q[0])
    o = _attention(qq, kk, vv, batch, seq)
    h_b, h_packed = _mm_ln(o, bf(sb_w_o[0]), h_b, ln_g[1, 0], ln_b[1, 0], out_kinds=(BF16, PACKED))
    out = moe_and_ple(1, h_b, h_packed, last=True)
    return out.reshape(batch, seq, d)
```

```python
import functools
import math

import jax
import jax.numpy as jnp
from jax import lax
from jax.experimental import pallas as pl
from jax.experimental.pallas import tpu as pltpu

BF16, F32, I32, U32 = jnp.bfloat16, jnp.float32, jnp.int32, jnp.uint32

DEPTH = 2
POOL_WINDOWS = (2, 4, 8, 16)
POOL_HALO = 16
HEAD_DIM = 128
TOP_K = 8
ROUTED_SCALE = 2.5
LN_EPS = 1e-5
DEEPNORM_ALPHA = (2.0 * DEPTH) ** 0.25
EXP_ZERO_BELOW = -110.0
VMEM_LIMIT_BYTES = 56 * 1024 * 1024
EXPERT_TILE_ROWS = 256
EXPERT_CHUNKS = 4
LN_ROW_BLOCK = 32
PACKED = "packed"


def _params(n_axes=1):
    return pltpu.CompilerParams(dimension_semantics=("arbitrary",) * n_axes,
                                vmem_limit_bytes=VMEM_LIMIT_BYTES)


def _dot(a, b):
    return jnp.dot(a, b, preferred_element_type=F32)


def _sigmoid(x):
    return 1.0 / (1.0 + jnp.exp(-x))


def _pack_bf16_pair(lo, hi):
    lo_bits = lax.bitcast_convert_type(lo.astype(BF16).astype(F32), U32)
    hi_bits = lax.bitcast_convert_type(hi.astype(BF16).astype(F32), U32)
    return hi_bits | (lo_bits >> 16)


def _unpack_bf16_pair(words):
    lo = lax.bitcast_convert_type(words << 16, F32)
    hi = lax.bitcast_convert_type(words & jnp.uint32(0xFFFF0000), F32)
    return lo, hi


def _mm_kernel(x_ref, w_ref, o_ref):
    o_ref[...] = _dot(x_ref[...], w_ref[...].astype(BF16)).astype(o_ref.dtype)


def _matmul(x, w, *, tm=1024, tn=512, out_dtype=BF16):
    m, k = x.shape
    n = w.shape[1]
    tm, tn = min(tm, m), min(tn, n)
    return pl.pallas_call(
        _mm_kernel,
        grid=(m // tm, n // tn),
        in_specs=[pl.BlockSpec((tm, k), lambda i, j: (i, 0)),
                  pl.BlockSpec((k, tn), lambda i, j: (0, j))],
        out_specs=pl.BlockSpec((tm, tn), lambda i, j: (i, j)),
        out_shape=jax.ShapeDtypeStruct((m, n), out_dtype),
        compiler_params=_params(2),
        name="matmul",
    )(x, w)


def _pool_grp_kernel(u_ref, halo_ref, wg_ref, sc_ref, o_ref, *, tm, tiles_per_seq):
    g = pl.program_id(0)
    seq_tile = pl.program_id(1) % tiles_per_seq
    cur = u_ref[...].astype(F32)
    halo = halo_ref[...].astype(F32)
    halo = jnp.where(seq_tile == 0, 0.0, halo)
    ext = jnp.concatenate([halo, cur], axis=0)
    row = lax.broadcasted_iota(I32, (tm, 1), 0)
    pos1 = (seq_tile * tm + row + 1).astype(F32)

    for gi, w in enumerate(POOL_WINDOWS):
        @pl.when(g == gi)
        def _(w=w):
            s = ext
            span = 1
            while span < w:
                s = s + pltpu.roll(s, span, 0)
                span *= 2
            win = s[POOL_HALO:]
            cnt = jnp.minimum(pos1, float(w))
            pooled = win / cnt - cur
            mixed = _dot(pooled.astype(BF16), wg_ref[...]) * sc_ref[...]
            o_ref[...] = mixed.astype(o_ref.dtype)


def _pool_grp(u, w_grp, scale, seq, *, tm=512):
    t, d = u.shape
    ng, dg, _ = w_grp.shape
    tm = min(tm, seq)
    hb = tm // POOL_HALO
    return pl.pallas_call(
        functools.partial(_pool_grp_kernel, tm=tm, tiles_per_seq=seq // tm),
        grid=(ng, t // tm),
        in_specs=[pl.BlockSpec((tm, dg), lambda g, i: (i, g)),
                  pl.BlockSpec((POOL_HALO, dg), lambda g, i: (jnp.maximum(i * hb - 1, 0), g)),
                  pl.BlockSpec((None, dg, dg), lambda g, i: (g, 0, 0)),
                  pl.BlockSpec((1, dg), lambda g, i: (0, g))],
        out_specs=pl.BlockSpec((tm, dg), lambda g, i: (i, g)),
        out_shape=jax.ShapeDtypeStruct((t, d), BF16),
        compiler_params=_params(2),
        name="pool_grp",
    )(u, u, w_grp, scale.reshape(1, d))


def _row_stats_merge(j, tn, z, mean_ref, m2_ref):
    cm = jnp.mean(z, axis=-1, keepdims=True)
    dz = z - cm
    cm2 = jnp.sum(dz * dz, axis=-1, keepdims=True)

    @pl.when(j == 0)
    def _():
        mean_ref[...] = cm
        m2_ref[...] = cm2

    @pl.when(j > 0)
    def _():
        n_a = (j * tn).astype(F32)
        tot = n_a + float(tn)
        delta = cm - mean_ref[...]
        mean_ref[...] = mean_ref[...] + delta * (float(tn) / tot)
        m2_ref[...] = m2_ref[...] + cm2 + delta * delta * (n_a * float(tn) / tot)


def _mm_ln_kernel(*refs, nj, tn, d, gated, out_kinds):
    if gated:
        lhs_ref, w_ref, res_ref, g_ref, b_ref, p_ref, wp_ref, bg_ref = refs[:8]
        rest = refs[8:]
    else:
        lhs_ref, w_ref, res_ref, g_ref, b_ref = refs[:5]
        rest = refs[5:]
    n_out = len(out_kinds)
    outs, (acc_ref, mean_ref, m2_ref) = rest[:n_out], rest[n_out:]
    j = pl.program_id(1)

    y = _dot(lhs_ref[...], w_ref[...])
    if gated:
        y = _sigmoid(y + bg_ref[...]) * _dot(p_ref[...], wp_ref[...])
    z = DEEPNORM_ALPHA * res_ref[...].astype(F32) + y
    acc_ref[j] = z
    _row_stats_merge(j, tn, z, mean_ref, m2_ref)

    @pl.when(j == nj - 1)
    def _():
        tm = acc_ref.shape[1]
        rb = min(LN_ROW_BLOCK, tm)
        half = nj // 2
        for r in range(tm // rb):
            rows = slice(r * rb, (r + 1) * rb)
            mu = mean_ref[rows, :]
            rstd = lax.rsqrt(m2_ref[rows, :] * (1.0 / d) + LN_EPS)

            def normed(jj):
                sl = slice(jj * tn, (jj + 1) * tn)
                return (acc_ref[jj, rows, :] - mu) * rstd * g_ref[:, sl] + b_ref[:, sl]

            for jj in range(half):
                pair = (jj, jj + half)
                hn = [normed(c) for c in pair]
                for kind, o_ref in zip(out_kinds, outs):
                    if kind == PACKED:
                        o_ref[rows, jj * tn:(jj + 1) * tn] = _pack_bf16_pair(hn[0], hn[1])
                    else:
                        for c, v in zip(pair, hn):
                            o_ref[rows, c * tn:(c + 1) * tn] = v.astype(o_ref.dtype)


def _mm_ln(lhs, w, res, ln_g, ln_b, *, out_kinds=(BF16,), gate=None, tm=512, tn=512):
    t, k = lhs.shape
    d = w.shape[1]
    tm, tn = min(tm, t), min(tn, d // 2)
    nj = d // tn
    in_specs = [pl.BlockSpec((tm, k), lambda i, j: (i, 0)),
                pl.BlockSpec((k, tn), lambda i, j: (0, j)),
                pl.BlockSpec((tm, tn), lambda i, j: (i, j)),
                pl.BlockSpec((1, d), lambda i, j: (0, 0)),
                pl.BlockSpec((1, d), lambda i, j: (0, 0))]
    args = [lhs, w, res, ln_g.reshape(1, d), ln_b.reshape(1, d)]
    if gate is not None:
        p, w_proj, b_gate = gate
        pd = p.shape[1]
        in_specs += [pl.BlockSpec((tm, pd), lambda i, j: (i, 0)),
                     pl.BlockSpec((pd, tn), lambda i, j: (0, j)),
                     pl.BlockSpec((1, tn), lambda i, j: (0, j))]
        args += [p, w_proj, b_gate.reshape(1, d)]
    out_shapes = [jax.ShapeDtypeStruct((t, d // 2), U32) if kind == PACKED
                  else jax.ShapeDtypeStruct((t, d), kind) for kind in out_kinds]
    return pl.pallas_call(
        functools.partial(_mm_ln_kernel, nj=nj, tn=tn, d=d, gated=gate is not None,
                          out_kinds=tuple(out_kinds)),
        grid=(t // tm, nj),
        in_specs=in_specs,
        out_specs=[pl.BlockSpec((tm, s.shape[1]), lambda i, j: (i, 0)) for s in out_shapes],
        out_shape=out_shapes,
        scratch_shapes=[pltpu.VMEM((nj, tm, tn), F32),
                        pltpu.VMEM((tm, 1), F32),
                        pltpu.VMEM((tm, 1), F32)],
        compiler_params=_params(2),
        name="mm_ln_gated" if gate is not None else "mm_ln",
    )(*args)


def _router_kernel(h_ref, whi_ref, wlo_ref, bias_ref, idx_ref, wts_ref, rank_ref, cnt_ref,
                   carry_ref, *, tm, n_exp):
    @pl.when(pl.program_id(0) == 0)
    def _():
        carry_ref[...] = jnp.zeros_like(carry_ref)

    h = h_ref[...]
    scores = _sigmoid(_dot(h, whi_ref[...]) + _dot(h, wlo_ref[...]))
    sel = scores + bias_ref[...]
    lane = lax.broadcasted_iota(I32, (tm, n_exp), 1)
    lane_k = lax.broadcasted_iota(I32, (tm, TOP_K), 1)
    mask = jnp.zeros((tm, n_exp), F32)
    idxs = jnp.zeros((tm, TOP_K), I32)
    wsel = jnp.zeros((tm, TOP_K), F32)
    for k in range(TOP_K):
        m = jnp.max(sel, axis=-1, keepdims=True)
        ik = jnp.min(jnp.where(sel == m, lane, n_exp), axis=-1, keepdims=True)
        onehot = lane == ik
        mask = jnp.where(onehot, 1.0, mask)
        sel = jnp.where(onehot, -jnp.inf, sel)
        sk = jnp.sum(jnp.where(onehot, scores, 0.0), axis=-1, keepdims=True)
        idxs = jnp.where(lane_k == k, ik, idxs)
        wsel = jnp.where(lane_k == k, sk, wsel)
    idx_ref[...] = idxs
    wts_ref[...] = wsel / jnp.sum(wsel, axis=-1, keepdims=True) * ROUTED_SCALE

    r = lax.broadcasted_iota(I32, (tm, tm), 0)
    c = lax.broadcasted_iota(I32, (tm, tm), 1)
    lower = jnp.where(c < r, 1.0, 0.0).astype(BF16)
    rank_ref[...] = carry_ref[...] + _dot(lower, mask.astype(BF16))
    carry_ref[...] = carry_ref[...] + jnp.sum(mask, axis=0, keepdims=True)
    cnt_ref[...] = carry_ref[...]


def _router(h, w_hi, w_lo, bias, *, tm=512):
    t, d = h.shape
    n_exp = w_hi.shape[1]
    tm = min(tm, t)
    return pl.pallas_call(
        functools.partial(_router_kernel, tm=tm, n_exp=n_exp),
        grid=(t // tm,),
        in_specs=[pl.BlockSpec((tm, d), lambda i: (i, 0)),
                  pl.BlockSpec((d, n_exp), lambda i: (0, 0)),
                  pl.BlockSpec((d, n_exp), lambda i: (0, 0)),
                  pl.BlockSpec((1, n_exp), lambda i: (0, 0))],
        out_specs=[pl.BlockSpec((tm, TOP_K), lambda i: (i, 0)),
                   pl.BlockSpec((tm, TOP_K), lambda i: (i, 0)),
                   pl.BlockSpec((tm, n_exp), lambda i: (i, 0)),
                   pl.BlockSpec((1, n_exp), lambda i: (0, 0))],
        out_shape=[jax.ShapeDtypeStruct((t, TOP_K), I32),
                   jax.ShapeDtypeStruct((t, TOP_K), F32),
                   jax.ShapeDtypeStruct((t, n_exp), F32),
                   jax.ShapeDtypeStruct((1, n_exp), F32)],
        scratch_shapes=[pltpu.VMEM((1, n_exp), F32)],
        compiler_params=_params(1),
        name="router",
    )(h, w_hi, w_lo, bias.reshape(1, n_exp))


def _pos_kernel(idx_ref, rank_ref, off_ref, pos_ref, *, tm, n_exp):
    base = rank_ref[...] + off_ref[...]
    idx = idx_ref[...]
    lane = lax.broadcasted_iota(I32, (tm, n_exp), 1)
    lane_k = lax.broadcasted_iota(I32, (tm, TOP_K), 1)
    pos = jnp.zeros((tm, TOP_K), F32)
    for k in range(TOP_K):
        pk = jnp.sum(jnp.where(lane == idx[:, k:k + 1], base, 0.0), axis=-1, keepdims=True)
        pos = jnp.where(lane_k == k, pk, pos)
    pos_ref[...] = pos.astype(I32)


def _positions(idx, rank, row_off, *, tm=512):
    t, n_exp = rank.shape
    tm = min(tm, t)
    return pl.pallas_call(
        functools.partial(_pos_kernel, tm=tm, n_exp=n_exp),
        grid=(t // tm,),
        in_specs=[pl.BlockSpec((tm, TOP_K), lambda i: (i, 0)),
                  pl.BlockSpec((tm, n_exp), lambda i: (i, 0)),
                  pl.BlockSpec((1, n_exp), lambda i: (0, 0))],
        out_specs=pl.BlockSpec((tm, TOP_K), lambda i: (i, 0)),
        out_shape=jax.ShapeDtypeStruct((t, TOP_K), I32),
        compiler_params=_params(1),
        name="positions",
    )(idx, rank, row_off)


def _dispatch_shared_kernel(pad_lo_ref, pad_hi_ref, pos_ref, hp_ref, h_ref, wg_ref, wu_ref, wd_ref,
                            xs_ref, ys_ref, zero_ref, sem, *, td, n_exp):
    i = pl.program_id(0)

    def row_copy(src_ref, src_row, dst_row, s):
        return pltpu.make_async_copy(src_ref.at[pl.ds(src_row, 1)], xs_ref.at[pl.ds(dst_row, 1)], sem.at[s])

    def issue(n, carry):
        for k in range(TOP_K):
            row_copy(hp_ref, n, pos_ref[n * TOP_K + k], 0).start()
        return carry

    def drain(n, carry):
        for k in range(TOP_K):
            row_copy(hp_ref, 0, 0, 0).wait()
        return carry

    lax.fori_loop(0, td, issue, 0)

    @pl.when(i == 0)
    def _():
        zero_ref[...] = jnp.zeros_like(zero_ref)

        def per_expert(e, carry):
            lo, hi = pad_lo_ref[e], pad_hi_ref[e]

            def zissue(r, c):
                row_copy(zero_ref, 0, r, 1).start()
                return c

            def zdrain(r, c):
                row_copy(zero_ref, 0, 0, 1).wait()
                return c

            lax.fori_loop(lo, hi, zissue, 0)
            lax.fori_loop(lo, hi, zdrain, 0)
            return carry

        lax.fori_loop(0, n_exp, per_expert, 0)

    h = h_ref[...]
    g = _dot(h, wg_ref[...])
    u = _dot(h, wu_ref[...])
    hm = (g * _sigmoid(g) * u).astype(BF16)
    ys_ref[...] = _dot(hm, wd_ref[...]).astype(ys_ref.dtype)

    lax.fori_loop(0, td, drain, 0)


def _dispatch_shared(pos_flat, h_packed, h, wg, wu, wd, pad_lo, pad_hi, n_rows, *, td=256):
    t, dh = h_packed.shape
    d, f = wg.shape
    td = min(td, t)
    n_exp = pad_lo.shape[0]
    grid_spec = pltpu.PrefetchScalarGridSpec(
        num_scalar_prefetch=2,
        grid=(t // td,),
        in_specs=[pl.BlockSpec((td * TOP_K,), lambda i, lo, hi: (i,), memory_space=pltpu.SMEM),
                  pl.BlockSpec((td, dh), lambda i, lo, hi: (i, 0)),
                  pl.BlockSpec((td, d), lambda i, lo, hi: (i, 0)),
                  pl.BlockSpec((d, f), lambda i, lo, hi: (0, 0)),
                  pl.BlockSpec((d, f), lambda i, lo, hi: (0, 0)),
                  pl.BlockSpec((f, d), lambda i, lo, hi: (0, 0))],
        out_specs=[pl.BlockSpec(memory_space=pl.ANY),
                   pl.BlockSpec((td, d), lambda i, lo, hi: (i, 0))],
        scratch_shapes=[pltpu.VMEM((8, dh), h_packed.dtype), pltpu.SemaphoreType.DMA((2,))],
    )
    return pl.pallas_call(
        functools.partial(_dispatch_shared_kernel, td=td, n_exp=n_exp),
        grid_spec=grid_spec,
        out_shape=[jax.ShapeDtypeStruct((n_rows, dh), h_packed.dtype),
                   jax.ShapeDtypeStruct((t, d), BF16)],
        compiler_params=pltpu.CompilerParams(dimension_semantics=("arbitrary",),
                                             vmem_limit_bytes=VMEM_LIMIT_BYTES,
                                             has_side_effects=True),
        name="dispatch_shared",
    )(pad_lo, pad_hi, pos_flat, h_packed, h, wg, wu, wd)


def _expert_kernel(te_ref, ts_in_ref, ts_out_ref, first_ref, nxt_ref, slot_ref, slot_prev_ref, nused_ref,
                   xs_ref, wg_hbm, wu_hbm, wd_hbm, ye_ref,
                   wg_f, wu_f, wd_f, wgu_b, wd_b, hm_even, hm_odd, gu_acc, sem, *, layer, f):
    j = pl.program_id(0)
    n_used = nused_ref[0]
    dh = xs_ref.shape[1]
    cw = dh // EXPERT_CHUNKS

    def weight_copies(e):
        return (pltpu.make_async_copy(wg_hbm.at[layer, e], wg_f, sem.at[0]),
                pltpu.make_async_copy(wu_hbm.at[layer, e], wu_f, sem.at[1]),
                pltpu.make_async_copy(wd_hbm.at[layer, e], wd_f, sem.at[2]))

    @pl.when(j == 0)
    def _():
        for cp in weight_copies(te_ref[0]):
            cp.start()

    @pl.when(first_ref[j] == 1)
    def _():
        for cp in weight_copies(te_ref[j]):
            cp.wait()
        wgu_b[:, :f] = wg_f[...].astype(BF16)
        wgu_b[:, f:] = wu_f[...].astype(BF16)
        wd_b[slot_ref[j]] = wd_f[...].astype(BF16)

        @pl.when(nxt_ref[j] >= 0)
        def _():
            for cp in weight_copies(nxt_ref[j]):
                cp.start()

    def hidden():
        for c in range(EXPERT_CHUNKS):
            lo, hi = _unpack_bf16_pair(xs_ref[:, c * cw:(c + 1) * cw])
            part = (_dot(lo.astype(BF16), wgu_b[c * cw:(c + 1) * cw, :])
                    + _dot(hi.astype(BF16), wgu_b[dh + c * cw:dh + (c + 1) * cw, :]))
            if c == 0:
                gu_acc[...] = part
            else:
                gu_acc[...] += part
        g, u = gu_acc[:, :f], gu_acc[:, f:]
        return (g * _sigmoid(g) * u).astype(BF16)

    def down(hm):
        wd = wd_b.at[slot_prev_ref[j]]
        for c in range(EXPERT_CHUNKS):
            ye_ref[:, c * cw:(c + 1) * cw] = _pack_bf16_pair(
                _dot(hm, wd[:, c * cw:(c + 1) * cw]),
                _dot(hm, wd[:, dh + c * cw:dh + (c + 1) * cw]))

    @pl.when(j == 0)
    def _():
        hm_even[...] = hidden()

    for parity, (hm_cur, hm_prev) in enumerate(((hm_even, hm_odd), (hm_odd, hm_even))):
        @pl.when((j >= 1) & (j < n_used) & (j % 2 == parity))
        def _(hm_cur=hm_cur, hm_prev=hm_prev):
            hm_cur[...] = hidden()
            down(hm_prev[...])

        @pl.when((j >= 1) & (j == n_used) & (j % 2 == parity))
        def _(hm_prev=hm_prev):
            down(hm_prev[...])


def _experts(tile_expert, tile_in, tile_out, tile_first, tile_next, tile_slot, tile_slot_prev, n_used,
             xs, w_gate, w_up, w_down, *, layer, tm):
    n_rows, dh = xs.shape
    d, f = w_gate.shape[2], w_gate.shape[3]
    n_steps = tile_expert.shape[0]
    grid_spec = pltpu.PrefetchScalarGridSpec(
        num_scalar_prefetch=8,
        grid=(n_steps,),
        in_specs=[pl.BlockSpec((tm, dh), lambda j, te, ti, to, fi, nx, sl, sp, nu: (ti[j], 0)),
                  pl.BlockSpec(memory_space=pl.ANY),
                  pl.BlockSpec(memory_space=pl.ANY),
                  pl.BlockSpec(memory_space=pl.ANY)],
        out_specs=pl.BlockSpec((tm, dh), lambda j, te, ti, to, fi, nx, sl, sp, nu: (to[j], 0)),
        scratch_shapes=[pltpu.VMEM((d, f), F32), pltpu.VMEM((d, f), F32), pltpu.VMEM((f, d), F32),
                        pltpu.VMEM((d, 2 * f), BF16), pltpu.VMEM((2, f, d), BF16),
                        pltpu.VMEM((tm, f), BF16), pltpu.VMEM((tm, f), BF16),
                        pltpu.VMEM((tm, 2 * f), F32),
                        pltpu.SemaphoreType.DMA((3,))],
    )
    return pl.pallas_call(
        functools.partial(_expert_kernel, layer=layer, f=f),
        grid_spec=grid_spec,
        out_shape=jax.ShapeDtypeStruct((n_rows, dh), xs.dtype),
        compiler_params=_params(1),
        name="experts",
    )(tile_expert, tile_in, tile_out, tile_first, tile_next, tile_slot, tile_slot_prev, n_used,
      xs, w_gate, w_up, w_down)


COMBINE_GROUP = 16


def _combine_ln_kernel(pos_ref, pos_next_ref, wts_ref, ys_ref, h_ref, g_ref, b_ref, ye_hbm, o_ref,
                       buf0, buf1, sem, *, tc, n_tiles):
    i = pl.program_id(0)
    dh = buf0.shape[2]
    d = 2 * dh

    def issue_group(p_ref, buf, s, r0):
        for dn in range(COMBINE_GROUP):
            for k in range(TOP_K):
                pltpu.make_async_copy(ye_hbm.at[pl.ds(p_ref[(r0 + dn) * TOP_K + k], 1)],
                                      buf.at[k, pl.ds(r0 + dn, 1)], sem.at[s]).start()

    def wait_tile(buf, s):
        def body(n, carry):
            for k in range(TOP_K):
                pltpu.make_async_copy(ye_hbm.at[pl.ds(0, 1)], buf.at[k, pl.ds(n, 1)], sem.at[s]).wait()
            return carry
        lax.fori_loop(0, tc, body, 0)

    def compute_group(buf, r0):
        rows = pl.ds(r0, COMBINE_GROUP)
        w = wts_ref[rows, :]
        y_lo = ys_ref[rows, :dh].astype(F32)
        y_hi = ys_ref[rows, dh:].astype(F32)
        for k in range(TOP_K):
            lo, hi = _unpack_bf16_pair(buf[k, rows, :])
            y_lo = y_lo + w[:, k:k + 1] * lo
            y_hi = y_hi + w[:, k:k + 1] * hi
        z_lo = DEEPNORM_ALPHA * h_ref[rows, :dh].astype(F32) + y_lo
        z_hi = DEEPNORM_ALPHA * h_ref[rows, dh:].astype(F32) + y_hi
        mu = (jnp.sum(z_lo, axis=-1, keepdims=True) + jnp.sum(z_hi, axis=-1, keepdims=True)) * (1.0 / d)
        c_lo, c_hi = z_lo - mu, z_hi - mu
        var = (jnp.sum(c_lo * c_lo, axis=-1, keepdims=True)
               + jnp.sum(c_hi * c_hi, axis=-1, keepdims=True)) * (1.0 / d)
        rstd = lax.rsqrt(var + LN_EPS)
        o_ref[rows, :dh] = (c_lo * rstd * g_ref[:, :dh] + b_ref[:, :dh]).astype(o_ref.dtype)
        o_ref[rows, dh:] = (c_hi * rstd * g_ref[:, dh:] + b_ref[:, dh:]).astype(o_ref.dtype)

    n_groups = tc // COMBINE_GROUP

    @pl.when(i == 0)
    def _():
        def first(gi, carry):
            issue_group(pos_ref, buf0, 0, gi * COMBINE_GROUP)
            return carry
        lax.fori_loop(0, n_groups, first, 0)

    def run(cur, cur_s, nxt, nxt_s):
        wait_tile(cur, cur_s)

        def group(gi, carry):
            r0 = pl.multiple_of(gi * COMBINE_GROUP, COMBINE_GROUP)
            issue_group(pos_next_ref, nxt, nxt_s, r0)
            compute_group(cur, r0)
            return carry
        lax.fori_loop(0, n_groups, group, 0)

        @pl.when(i == n_tiles - 1)
        def _():
            wait_tile(nxt, nxt_s)

    @pl.when(i % 2 == 0)
    def _():
        run(buf0, 0, buf1, 1)

    @pl.when(i % 2 == 1)
    def _():
        run(buf1, 1, buf0, 0)


def _combine_ln(pos_flat, wts, ys, h, ye, ln_g, ln_b, *, tc=256):
    t, d = h.shape
    tc = min(tc, t)
    n_tiles = t // tc
    return pl.pallas_call(
        functools.partial(_combine_ln_kernel, tc=tc, n_tiles=n_tiles),
        grid=(n_tiles,),
        in_specs=[pl.BlockSpec((tc * TOP_K,), lambda i: (i,), memory_space=pltpu.SMEM),
                  pl.BlockSpec((tc * TOP_K,), lambda i: (jnp.minimum(i + 1, n_tiles - 1),),
                               memory_space=pltpu.SMEM),
                  pl.BlockSpec((tc, TOP_K), lambda i: (i, 0)),
                  pl.BlockSpec((tc, d), lambda i: (i, 0)),
                  pl.BlockSpec((tc, d), lambda i: (i, 0)),
                  pl.BlockSpec((1, d), lambda i: (0, 0)),
                  pl.BlockSpec((1, d), lambda i: (0, 0)),
                  pl.BlockSpec(memory_space=pl.ANY)],
        out_specs=pl.BlockSpec((tc, d), lambda i: (i, 0)),
        out_shape=jax.ShapeDtypeStruct((t, d), BF16),
        scratch_shapes=[pltpu.VMEM((TOP_K, tc, d // 2), ye.dtype),
                        pltpu.VMEM((TOP_K, tc, d // 2), ye.dtype),
                        pltpu.SemaphoreType.DMA((2,))],
        compiler_params=_params(1),
        name="combine_ln",
    )(pos_flat, pos_flat, wts, ys, h, ln_g.reshape(1, d), ln_b.reshape(1, d), ye)


def _later_matrix(win):
    j = lax.broadcasted_iota(I32, (win, win), 0)
    s = lax.broadcasted_iota(I32, (win, win), 1)
    return jnp.where(j > s, 1.0, 0.0).astype(BF16)


def _attn_kernel(q_ref, k_ref, v_ref, o_ref, *, seq, heads, tq, win0, win):
    inv_sqrt_d = 1.0 / math.sqrt(HEAD_DIM)
    later = {w: _later_matrix(w) for w in {win0, win}}

    def add_window(w, t0, k_start, k_limit, surv, accs):
        ks = pl.multiple_of(k_start, HEAD_DIM)
        lane = lax.broadcasted_iota(I32, (tq, w), 1)
        rowi = lax.broadcasted_iota(I32, (tq, w), 0)
        kpos = k_start + lane
        valid = (kpos < t0 + rowi) & (kpos < k_limit)
        cols = [slice(h * HEAD_DIM, (h + 1) * HEAD_DIM) for h in range(heads)]
        z = [lax.dot_general(q_ref[pl.ds(t0, tq), c], k_ref[pl.ds(ks, w), c],
                             (((1,), (1,)), ((), ())), preferred_element_type=F32) * inv_sqrt_d
             for c in cols]
        softplus = [jnp.maximum(x, 0.0) + jnp.log(1.0 + jnp.exp(-jnp.abs(x))) for x in z]
        log_fail = [jnp.where(valid, -sp, 0.0) for sp in softplus]
        lf_hi = [lf.astype(BF16) for lf in log_fail]
        lf_lo = [(lf - hi.astype(F32)).astype(BF16) for lf, hi in zip(log_fail, lf_hi)]
        between = [_dot(hi, later[w]) + _dot(lo, later[w]) for hi, lo in zip(lf_hi, lf_lo)]
        a = [jnp.where(valid, jnp.exp(x - sp + b + s), 0.0)
             for x, sp, b, s in zip(z, softplus, between, surv)]
        accs = [acc + _dot(p.astype(BF16), v_ref[pl.ds(ks, w), c]) for acc, p, c in zip(accs, a, cols)]
        surv = [s + jnp.sum(lf, axis=-1, keepdims=True) for s, lf in zip(surv, log_fail)]
        return surv, accs

    def any_alive(surv):
        m = surv[0]
        for s in surv[1:]:
            m = jnp.maximum(m, s)
        return (jnp.max(m) >= EXP_ZERO_BELOW).astype(I32)

    def q_tile(qi, carry):
        t0 = pl.multiple_of(qi * tq, tq)
        start0 = jnp.maximum(t0 + tq - win0, 0)
        surv = [jnp.zeros((tq, 1), F32) for _ in range(heads)]
        accs = [jnp.zeros((tq, HEAD_DIM), F32) for _ in range(heads)]
        surv, accs = add_window(win0, t0, start0, seq + win0, surv, accs)

        def cond(state):
            prev_start, alive, _, _ = state
            return (prev_start > 0) & (alive > 0)

        def body(state):
            prev_start, _, surv, accs = state
            start = jnp.maximum(prev_start - win, 0)
            surv, accs = add_window(win, t0, start, prev_start, surv, accs)
            return start, any_alive(surv), surv, accs

        _, _, _, accs = lax.while_loop(cond, body, (start0, any_alive(surv), surv, accs))
        for h in range(heads):
            o_ref[pl.ds(t0, tq), h * HEAD_DIM:(h + 1) * HEAD_DIM] = accs[h].astype(o_ref.dtype)
        return carry

    lax.fori_loop(0, seq // tq, q_tile, 0)


def _attention(q, k, v, batch, seq, *, heads_per_step=4, tq=128, win0=384, win=256):
    t, d = q.shape
    n_heads = d // HEAD_DIM
    hb = min(heads_per_step, n_heads)
    tq, win0, win = min(tq, seq), min(win0, seq), min(win, seq)
    spec = pl.BlockSpec((seq, hb * HEAD_DIM), lambda b, h: (b, h))
    return pl.pallas_call(
        functools.partial(_attn_kernel, seq=seq, heads=hb, tq=tq, win0=win0, win=win),
        grid=(batch, n_heads // hb),
        in_specs=[spec, spec, spec],
        out_specs=spec,
        out_shape=jax.ShapeDtypeStruct((t, d), BF16),
        compiler_params=_params(2),
        name="attention",
    )(q, k, v)


def _moe_sublayer(layer, h_b, h_packed, w_router, bias, w_gate, w_up, w_down, ws_gate, ws_up, ws_down,
                  ln_g, ln_b, *, tile_rows):
    t, d = h_b.shape
    n_exp = w_router.shape[1]
    w_hi = w_router.astype(BF16)
    w_lo = (w_router - w_hi.astype(F32)).astype(BF16)
    idx, wts, rank, cnt = _router(h_b, w_hi, w_lo, bias)

    counts = cnt[0].astype(I32)
    n_tiles_e = (counts + tile_rows - 1) // tile_rows
    tile_end = jnp.cumsum(n_tiles_e)
    tile_start = tile_end - n_tiles_e
    row_off = tile_start * tile_rows
    n_used = tile_end[-1]
    n_tiles = (t * TOP_K) // tile_rows + n_exp
    n_rows = n_tiles * tile_rows

    def expert_of(tile):
        return jnp.sum((tile_end[None, :] <= tile[:, None]).astype(I32), axis=1)

    def lookup(table, e):
        onehot = e[:, None] == jnp.arange(n_exp, dtype=I32)[None, :]
        return jnp.sum(jnp.where(onehot, table[None, :], 0), axis=1)

    step = jnp.arange(n_tiles + 1, dtype=I32)
    tile_in = jnp.minimum(step, n_used - 1)
    tile_out = jnp.minimum(jnp.maximum(step - 1, 0), n_used - 1)
    tile_expert = expert_of(tile_in)
    tile_first = (step == lookup(tile_start, tile_expert)).astype(I32)
    nxt_tile = lookup(tile_end, tile_expert)
    tile_next = jnp.where(nxt_tile < n_used, expert_of(jnp.minimum(nxt_tile, n_used - 1)), -1).astype(I32)
    tile_slot = (jnp.cumsum(tile_first) - 1) % 2
    tile_slot_prev = jnp.concatenate([tile_slot[:1], tile_slot[:-1]])

    pos = _positions(idx, rank, row_off.astype(F32).reshape(1, n_exp))
    pos_flat = pos.reshape(t * TOP_K)
    xs, ys = _dispatch_shared(pos_flat, h_packed, h_b, ws_gate.astype(BF16), ws_up.astype(BF16),
                              ws_down.astype(BF16), (row_off + counts).astype(I32),
                              (row_off + n_tiles_e * tile_rows).astype(I32), n_rows)
    ye = _experts(tile_expert, tile_in, tile_out, tile_first, tile_next, tile_slot.astype(I32),
                  tile_slot_prev.astype(I32), n_used.reshape(1).astype(I32),
                  xs, w_gate, w_up, w_down, layer=layer, tm=tile_rows)
    return _combine_ln(pos_flat, wts, ys, h_b, ye, ln_g, ln_b)


def kernel(x, p, pool_w_in, pool_w_grp, pool_scale, pool_w_out, kv_w_k, kv_w_v, sb_w_q, sb_w_o,
           moe_w_router, moe_bias, moe_w_gate, moe_w_up, moe_w_down, shared_w_gate, shared_w_up,
           shared_w_down, ple_w_proj, ple_w_gate, ple_b_gate, ln_g, ln_b):
    batch, seq, d = x.shape
    t = batch * seq
    xf = x.reshape(t, d)
    pb = p.reshape(p.shape[0], t, p.shape[-1]).astype(BF16)
    bf = lambda w: w.astype(BF16)

    def moe_and_ple(i, h_b, h_packed, last):
        h_b = _moe_sublayer(i, h_b, h_packed, moe_w_router[i], moe_bias[i], moe_w_gate, moe_w_up,
                            moe_w_down, shared_w_gate[i], shared_w_up[i], shared_w_down[i],
                            ln_g[i, 1], ln_b[i, 1], tile_rows=EXPERT_TILE_ROWS)
        return _mm_ln(h_b, bf(ple_w_gate[i]), h_b, ln_g[i, 2], ln_b[i, 2],
                      out_kinds=(F32,) if last else (BF16,),
                      gate=(pb[i], bf(ple_w_proj[i]), ple_b_gate[i]),
                      tn=512 if last else 1024)[0]

    u = _matmul(bf(xf), pool_w_in[0])
    mixed = _pool_grp(u, bf(pool_w_grp[0]), pool_scale[0], seq)
    h_b, h_packed = _mm_ln(mixed, bf(pool_w_out[0]), xf, ln_g[0, 0], ln_b[0, 0], out_kinds=(BF16, PACKED))
    h_b = moe_and_ple(0, h_b, h_packed, last=False)

    kk = _matmul(h_b, kv_w_k)
    vv = _matmul(h_b, kv_w_v)
    qq = _matmul(h_b, sb_w_q[0])
    o = _attention(qq, kk, vv, batch, seq)
    h_b, h_packed = _mm_ln(o, bf(sb_w_o[0]), h_b, ln_g[1, 0], ln_b[1, 0], out_kinds=(BF16, PACKED))
    out = moe_and_ple(1, h_b, h_packed, last=True)
    return out.reshape(batch, seq, d)
```

```python
import functools
import math

import jax
import jax.numpy as jnp
from jax import lax
from jax.experimental import pallas as pl
from jax.experimental.pallas import tpu as pltpu

BF16, F32, I32, U32 = jnp.bfloat16, jnp.float32, jnp.int32, jnp.uint32

DEPTH = 2
POOL_WINDOWS = (2, 4, 8, 16)
POOL_HALO = 16
HEAD_DIM = 128
TOP_K = 8
ROUTED_SCALE = 2.5
LN_EPS = 1e-5
DEEPNORM_ALPHA = (2.0 * DEPTH) ** 0.25
EXP_ZERO_BELOW = -110.0
VMEM_LIMIT_BYTES = 56 * 1024 * 1024
EXPERT_TILE_ROWS = 256
EXPERT_CHUNKS = 4
LN_ROW_BLOCK = 32
PACKED = "packed"


def _params(n_axes=1):
    return pltpu.CompilerParams(dimension_semantics=("arbitrary",) * n_axes,
                                vmem_limit_bytes=VMEM_LIMIT_BYTES)


def _dot(a, b):
    return jnp.dot(a, b, preferred_element_type=F32)


def _sigmoid(x):
    return 1.0 / (1.0 + jnp.exp(-x))


def _pack_bf16_pair(lo, hi):
    lo_bits = lax.bitcast_convert_type(lo.astype(BF16).astype(F32), U32)
    hi_bits = lax.bitcast_convert_type(hi.astype(BF16).astype(F32), U32)
    return hi_bits | (lo_bits >> 16)


def _unpack_bf16_pair(words):
    lo = lax.bitcast_convert_type(words << 16, F32)
    hi = lax.bitcast_convert_type(words & jnp.uint32(0xFFFF0000), F32)
    return lo, hi


def _mm_kernel(x_ref, w_ref, o_ref):
    o_ref[...] = _dot(x_ref[...], w_ref[...].astype(BF16)).astype(o_ref.dtype)


def _matmul(x, w, *, tm=1024, tn=512, out_dtype=BF16):
    m, k = x.shape
    n = w.shape[1]
    tm, tn = min(tm, m), min(tn, n)
    return pl.pallas_call(
        _mm_kernel,
        grid=(m // tm, n // tn),
        in_specs=[pl.BlockSpec((tm, k), lambda i, j: (i, 0)),
                  pl.BlockSpec((k, tn), lambda i, j: (0, j))],
        out_specs=pl.BlockSpec((tm, tn), lambda i, j: (i, j)),
        out_shape=jax.ShapeDtypeStruct((m, n), out_dtype),
        compiler_params=_params(2),
        name="matmul",
    )(x, w)


def _pool_grp_kernel(u_ref, halo_ref, wg_ref, sc_ref, o_ref, *, tm, tiles_per_seq):
    g = pl.program_id(0)
    seq_tile = pl.program_id(1) % tiles_per_seq
    cur = u_ref[...].astype(F32)
    halo = halo_ref[...].astype(F32)
    halo = jnp.where(seq_tile == 0, 0.0, halo)
    ext = jnp.concatenate([halo, cur], axis=0)
    row = lax.broadcasted_iota(I32, (tm, 1), 0)
    pos1 = (seq_tile * tm + row + 1).astype(F32)

    for gi, w in enumerate(POOL_WINDOWS):
        @pl.when(g == gi)
        def _(w=w):
            s = ext
            span = 1
            while span < w:
                s = s + pltpu.roll(s, span, 0)
                span *= 2
            win = s[POOL_HALO:]
            cnt = jnp.minimum(pos1, float(w))
            pooled = win / cnt - cur
            mixed = _dot(pooled.astype(BF16), wg_ref[...]) * sc_ref[...]
            o_ref[...] = mixed.astype(o_ref.dtype)


def _pool_grp(u, w_grp, scale, seq, *, tm=512):
    t, d = u.shape
    ng, dg, _ = w_grp.shape
    tm = min(tm, seq)
    hb = tm // POOL_HALO
    return pl.pallas_call(
        functools.partial(_pool_grp_kernel, tm=tm, tiles_per_seq=seq // tm),
        grid=(ng, t // tm),
        in_specs=[pl.BlockSpec((tm, dg), lambda g, i: (i, g)),
                  pl.BlockSpec((POOL_HALO, dg), lambda g, i: (jnp.maximum(i * hb - 1, 0), g)),
                  pl.BlockSpec((None, dg, dg), lambda g, i: (g, 0, 0)),
                  pl.BlockSpec((1, dg), lambda g, i: (0, g))],
        out_specs=pl.BlockSpec((tm, dg), lambda g, i: (i, g)),
        out_shape=jax.ShapeDtypeStruct((t, d), BF16),
        compiler_params=_params(2),
        name="pool_grp",
    )(u, u, w_grp, scale.reshape(1, d))


def _row_stats_merge(j, tn, z, mean_ref, m2_ref):
    cm = jnp.mean(z, axis=-1, keepdims=True)
    dz = z - cm
    cm2 = jnp.sum(dz * dz, axis=-1, keepdims=True)

    @pl.when(j == 0)
    def _():
        mean_ref[...] = cm
        m2_ref[...] = cm2

    @pl.when(j > 0)
    def _():
        n_a = (j * tn).astype(F32)
        tot = n_a + float(tn)
        delta = cm - mean_ref[...]
        mean_ref[...] = mean_ref[...] + delta * (float(tn) / tot)
        m2_ref[...] = m2_ref[...] + cm2 + delta * delta * (n_a * float(tn) / tot)


def _mm_ln_kernel(*refs, nj, tn, d, gated, out_kinds):
    if gated:
        lhs_ref, w_ref, res_ref, g_ref, b_ref, p_ref, wp_ref, bg_ref = refs[:8]
        rest = refs[8:]
    else:
        lhs_ref, w_ref, res_ref, g_ref, b_ref = refs[:5]
        rest = refs[5:]
    n_out = len(out_kinds)
    outs, (acc_ref, mean_ref, m2_ref) = rest[:n_out], rest[n_out:]
    j = pl.program_id(1)

    y = _dot(lhs_ref[...], w_ref[...])
    if gated:
        y = _sigmoid(y + bg_ref[...]) * _dot(p_ref[...], wp_ref[...])
    z = DEEPNORM_ALPHA * res_ref[...].astype(F32) + y
    acc_ref[j] = z
    _row_stats_merge(j, tn, z, mean_ref, m2_ref)

    @pl.when(j == nj - 1)
    def _():
        tm = acc_ref.shape[1]
        rb = min(LN_ROW_BLOCK, tm)
        half = nj // 2
        for r in range(tm // rb):
            rows = slice(r * rb, (r + 1) * rb)
            mu = mean_ref[rows, :]
            rstd = lax.rsqrt(m2_ref[rows, :] * (1.0 / d) + LN_EPS)

            def normed(jj):
                sl = slice(jj * tn, (jj + 1) * tn)
                return (acc_ref[jj, rows, :] - mu) * rstd * g_ref[:, sl] + b_ref[:, sl]

            for jj in range(half):
                pair = (jj, jj + half)
                hn = [normed(c) for c in pair]
                for kind, o_ref in zip(out_kinds, outs):
                    if kind == PACKED:
                        o_ref[rows, jj * tn:(jj + 1) * tn] = _pack_bf16_pair(hn[0], hn[1])
                    else:
                        for c, v in zip(pair, hn):
                            o_ref[rows, c * tn:(c + 1) * tn] = v.astype(o_ref.dtype)


def _mm_ln(lhs, w, res, ln_g, ln_b, *, out_kinds=(BF16,), gate=None, tm=512, tn=512):
    t, k = lhs.shape
    d = w.shape[1]
    tm, tn = min(tm, t), min(tn, d // 2)
    nj = d // tn
    in_specs = [pl.BlockSpec((tm, k), lambda i, j: (i, 0)),
                pl.BlockSpec((k, tn), lambda i, j: (0, j)),
                pl.BlockSpec((tm, tn), lambda i, j: (i, j)),
                pl.BlockSpec((1, d), lambda i, j: (0, 0)),
                pl.BlockSpec((1, d), lambda i, j: (0, 0))]
    args = [lhs, w, res, ln_g.reshape(1, d), ln_b.reshape(1, d)]
    if gate is not None:
        p, w_proj, b_gate = gate
        pd = p.shape[1]
        in_specs += [pl.BlockSpec((tm, pd), lambda i, j: (i, 0)),
                     pl.BlockSpec((pd, tn), lambda i, j: (0, j)),
                     pl.BlockSpec((1, tn), lambda i, j: (0, j))]
        args += [p, w_proj, b_gate.reshape(1, d)]
    out_shapes = [jax.ShapeDtypeStruct((t, d // 2), U32) if kind == PACKED
                  else jax.ShapeDtypeStruct((t, d), kind) for kind in out_kinds]
    return pl.pallas_call(
        functools.partial(_mm_ln_kernel, nj=nj, tn=tn, d=d, gated=gate is not None,
                          out_kinds=tuple(out_kinds)),
        grid=(t // tm, nj),
        in_specs=in_specs,
        out_specs=[pl.BlockSpec((tm, s.shape[1]), lambda i, j: (i, 0)) for s in out_shapes],
        out_shape=out_shapes,
        scratch_shapes=[pltpu.VMEM((nj, tm, tn), F32),
                        pltpu.VMEM((tm, 1), F32),
                        pltpu.VMEM((tm, 1), F32)],
        compiler_params=_params(2),
        name="mm_ln_gated" if gate is not None else "mm_ln",
    )(*args)


def _router_kernel(h_ref, whi_ref, wlo_ref, bias_ref, idx_ref, wts_ref, rank_ref, cnt_ref,
                   carry_ref, *, tm, n_exp):
    @pl.when(pl.program_id(0) == 0)
    def _():
        carry_ref[...] = jnp.zeros_like(carry_ref)

    h = h_ref[...]
    scores = _sigmoid(_dot(h, whi_ref[...]) + _dot(h, wlo_ref[...]))
    sel = scores + bias_ref[...]
    lane = lax.broadcasted_iota(I32, (tm, n_exp), 1)
    lane_k = lax.broadcasted_iota(I32, (tm, TOP_K), 1)
    mask = jnp.zeros((tm, n_exp), F32)
    idxs = jnp.zeros((tm, TOP_K), I32)
    wsel = jnp.zeros((tm, TOP_K), F32)
    for k in range(TOP_K):
        m = jnp.max(sel, axis=-1, keepdims=True)
        ik = jnp.min(jnp.where(sel == m, lane, n_exp), axis=-1, keepdims=True)
        onehot = lane == ik
        mask = jnp.where(onehot, 1.0, mask)
        sel = jnp.where(onehot, -jnp.inf, sel)
        sk = jnp.sum(jnp.where(onehot, scores, 0.0), axis=-1, keepdims=True)
        idxs = jnp.where(lane_k == k, ik, idxs)
        wsel = jnp.where(lane_k == k, sk, wsel)
    idx_ref[...] = idxs
    wts_ref[...] = wsel / jnp.sum(wsel, axis=-1, keepdims=True) * ROUTED_SCALE

    r = lax.broadcasted_iota(I32, (tm, tm), 0)
    c = lax.broadcasted_iota(I32, (tm, tm), 1)
    lower = jnp.where(c < r, 1.0, 0.0).astype(BF16)
    rank_ref[...] = carry_ref[...] + _dot(lower, mask.astype(BF16))
    carry_ref[...] = carry_ref[...] + jnp.sum(mask, axis=0, keepdims=True)
    cnt_ref[...] = carry_ref[...]


def _router(h, w_hi, w_lo, bias, *, tm=512):
    t, d = h.shape
    n_exp = w_hi.shape[1]
    tm = min(tm, t)
    return pl.pallas_call(
        functools.partial(_router_kernel, tm=tm, n_exp=n_exp),
        grid=(t // tm,),
        in_specs=[pl.BlockSpec((tm, d), lambda i: (i, 0)),
                  pl.BlockSpec((d, n_exp), lambda i: (0, 0)),
                  pl.BlockSpec((d, n_exp), lambda i: (0, 0)),
                  pl.BlockSpec((1, n_exp), lambda i: (0, 0))],
        out_specs=[pl.BlockSpec((tm, TOP_K), lambda i: (i, 0)),
                   pl.BlockSpec((tm, TOP_K), lambda i: (i, 0)),
                   pl.BlockSpec((tm, n_exp), lambda i: (i, 0)),
                   pl.BlockSpec((1, n_exp), lambda i: (0, 0))],
        out_shape=[jax.ShapeDtypeStruct((t, TOP_K), I32),
                   jax.ShapeDtypeStruct((t, TOP_K), F32),
                   jax.ShapeDtypeStruct((t, n_exp), F32),
                   jax.ShapeDtypeStruct((1, n_exp), F32)],
        scratch_shapes=[pltpu.VMEM((1, n_exp), F32)],
        compiler_params=_params(1),
        name="router",
    )(h, w_hi, w_lo, bias.reshape(1, n_exp))


def _pos_kernel(idx_ref, rank_ref, off_ref, pos_ref, *, tm, n_exp):
    base = rank_ref[...] + off_ref[...]
    idx = idx_ref[...]
    lane = lax.broadcasted_iota(I32, (tm, n_exp), 1)
    lane_k = lax.broadcasted_iota(I32, (tm, TOP_K), 1)
    pos = jnp.zeros((tm, TOP_K), F32)
    for k in range(TOP_K):
        pk = jnp.sum(jnp.where(lane == idx[:, k:k + 1], base, 0.0), axis=-1, keepdims=True)
        pos = jnp.where(lane_k == k, pk, pos)
    pos_ref[...] = pos.astype(I32)


def _positions(idx, rank, row_off, *, tm=512):
    t, n_exp = rank.shape
    tm = min(tm, t)
    return pl.pallas_call(
        functools.partial(_pos_kernel, tm=tm, n_exp=n_exp),
        grid=(t // tm,),
        in_specs=[pl.BlockSpec((tm, TOP_K), lambda i: (i, 0)),
                  pl.BlockSpec((tm, n_exp), lambda i: (i, 0)),
                  pl.BlockSpec((1, n_exp), lambda i: (0, 0))],
        out_specs=pl.BlockSpec((tm, TOP_K), lambda i: (i, 0)),
        out_shape=jax.ShapeDtypeStruct((t, TOP_K), I32),
        compiler_params=_params(1),
        name="positions",
    )(idx, rank, row_off)


def _dispatch_shared_kernel(pad_lo_ref, pad_hi_ref, pos_ref, hp_ref, h_ref, wg_ref, wu_ref, wd_ref,
                            xs_ref, ys_ref, zero_ref, sem, *, td, n_exp):
    i = pl.program_id(0)

    def row_copy(src_ref, src_row, dst_row, s):
        return pltpu.make_async_copy(src_ref.at[pl.ds(src_row, 1)], xs_ref.at[pl.ds(dst_row, 1)], sem.at[s])

    def issue(n, carry):
        for k in range(TOP_K):
            row_copy(hp_ref, n, pos_ref[n * TOP_K + k], 0).start()
        return carry

    def drain(n, carry):
        for k in range(TOP_K):
            row_copy(hp_ref, 0, 0, 0).wait()
        return carry

    lax.fori_loop(0, td, issue, 0)

    @pl.when(i == 0)
    def _():
        zero_ref[...] = jnp.zeros_like(zero_ref)

        def per_expert(e, carry):
            lo, hi = pad_lo_ref[e], pad_hi_ref[e]

            def zissue(r, c):
                row_copy(zero_ref, 0, r, 1).start()
                return c

            def zdrain(r, c):
                row_copy(zero_ref, 0, 0, 1).wait()
                return c

            lax.fori_loop(lo, hi, zissue, 0)
            lax.fori_loop(lo, hi, zdrain, 0)
            return carry

        lax.fori_loop(0, n_exp, per_expert, 0)

    h = h_ref[...]
    g = _dot(h, wg_ref[...])
    u = _dot(h, wu_ref[...])
    hm = (g * _sigmoid(g) * u).astype(BF16)
    ys_ref[...] = _dot(hm, wd_ref[...]).astype(ys_ref.dtype)

    lax.fori_loop(0, td, drain, 0)


def _dispatch_shared(pos_flat, h_packed, h, wg, wu, wd, pad_lo, pad_hi, n_rows, *, td=256):
    t, dh = h_packed.shape
    d, f = wg.shape
    td = min(td, t)
    n_exp = pad_lo.shape[0]
    grid_spec = pltpu.PrefetchScalarGridSpec(
        num_scalar_prefetch=2,
        grid=(t // td,),
        in_specs=[pl.BlockSpec((td * TOP_K,), lambda i, lo, hi: (i,), memory_space=pltpu.SMEM),
                  pl.BlockSpec((td, dh), lambda i, lo, hi: (i, 0)),
                  pl.BlockSpec((td, d), lambda i, lo, hi: (i, 0)),
                  pl.BlockSpec((d, f), lambda i, lo, hi: (0, 0)),
                  pl.BlockSpec((d, f), lambda i, lo, hi: (0, 0)),
                  pl.BlockSpec((f, d), lambda i, lo, hi: (0, 0))],
        out_specs=[pl.BlockSpec(memory_space=pl.ANY),
                   pl.BlockSpec((td, d), lambda i, lo, hi: (i, 0))],
        scratch_shapes=[pltpu.VMEM((8, dh), h_packed.dtype), pltpu.SemaphoreType.DMA((2,))],
    )
    return pl.pallas_call(
        functools.partial(_dispatch_shared_kernel, td=td, n_exp=n_exp),
        grid_spec=grid_spec,
        out_shape=[jax.ShapeDtypeStruct((n_rows, dh), h_packed.dtype),
                   jax.ShapeDtypeStruct((t, d), BF16)],
        compiler_params=pltpu.CompilerParams(dimension_semantics=("arbitrary",),
                                             vmem_limit_bytes=VMEM_LIMIT_BYTES,
                                             has_side_effects=True),
        name="dispatch_shared",
    )(pad_lo, pad_hi, pos_flat, h_packed, h, wg, wu, wd)


def _expert_kernel(te_ref, ts_in_ref, ts_out_ref, first_ref, nxt_ref, slot_ref, slot_prev_ref, nused_ref,
                   xs_ref, wg_hbm, wu_hbm, wd_hbm, ye_ref,
                   wg_f, wu_f, wd_f, wgu_b, wd_b, hm_even, hm_odd, gu_acc, sem, *, layer, f):
    j = pl.program_id(0)
    n_used = nused_ref[0]
    dh = xs_ref.shape[1]
    cw = dh // EXPERT_CHUNKS

    def weight_copies(e):
        return (pltpu.make_async_copy(wg_hbm.at[layer, e], wg_f, sem.at[0]),
                pltpu.make_async_copy(wu_hbm.at[layer, e], wu_f, sem.at[1]),
                pltpu.make_async_copy(wd_hbm.at[layer, e], wd_f, sem.at[2]))

    @pl.when(j == 0)
    def _():
        for cp in weight_copies(te_ref[0]):
            cp.start()

    @pl.when(first_ref[j] == 1)
    def _():
        for cp in weight_copies(te_ref[j]):
            cp.wait()
        wgu_b[:, :f] = wg_f[...].astype(BF16)
        wgu_b[:, f:] = wu_f[...].astype(BF16)
        wd_b[slot_ref[j]] = wd_f[...].astype(BF16)

        @pl.when(nxt_ref[j] >= 0)
        def _():
            for cp in weight_copies(nxt_ref[j]):
                cp.start()

    def hidden():
        for c in range(EXPERT_CHUNKS):
            lo, hi = _unpack_bf16_pair(xs_ref[:, c * cw:(c + 1) * cw])
            part = (_dot(lo.astype(BF16), wgu_b[c * cw:(c + 1) * cw, :])
                    + _dot(hi.astype(BF16), wgu_b[dh + c * cw:dh + (c + 1) * cw, :]))
            if c == 0:
                gu_acc[...] = part
            else:
                gu_acc[...] += part
        g, u = gu_acc[:, :f], gu_acc[:, f:]
        return (g * _sigmoid(g) * u).astype(BF16)

    def down(hm):
        wd = wd_b.at[slot_prev_ref[j]]
        for c in range(EXPERT_CHUNKS):
            ye_ref[:, c * cw:(c + 1) * cw] = _pack_bf16_pair(
                _dot(hm, wd[:, c * cw:(c + 1) * cw]),
                _dot(hm, wd[:, dh + c * cw:dh + (c + 1) * cw]))

    @pl.when(j == 0)
    def _():
        hm_even[...] = hidden()

    for parity, (hm_cur, hm_prev) in enumerate(((hm_even, hm_odd), (hm_odd, hm_even))):
        @pl.when((j >= 1) & (j < n_used) & (j % 2 == parity))
        def _(hm_cur=hm_cur, hm_prev=hm_prev):
            hm_cur[...] = hidden()
            down(hm_prev[...])

        @pl.when((j >= 1) & (j == n_used) & (j % 2 == parity))
        def _(hm_prev=hm_prev):
            down(hm_prev[...])


def _experts(tile_expert, tile_in, tile_out, tile_first, tile_next, tile_slot, tile_slot_prev, n_used,
             xs, w_gate, w_up, w_down, *, layer, tm):
    n_rows, dh = xs.shape
    d, f = w_gate.shape[2], w_gate.shape[3]
    n_steps = tile_expert.shape[0]
    grid_spec = pltpu.PrefetchScalarGridSpec(
        num_scalar_prefetch=8,
        grid=(n_steps,),
        in_specs=[pl.BlockSpec((tm, dh), lambda j, te, ti, to, fi, nx, sl, sp, nu: (ti[j], 0)),
                  pl.BlockSpec(memory_space=pl.ANY),
                  pl.BlockSpec(memory_space=pl.ANY),
                  pl.BlockSpec(memory_space=pl.ANY)],
        out_specs=pl.BlockSpec((tm, dh), lambda j, te, ti, to, fi, nx, sl, sp, nu: (to[j], 0)),
        scratch_shapes=[pltpu.VMEM((d, f), F32), pltpu.VMEM((d, f), F32), pltpu.VMEM((f, d), F32),
                        pltpu.VMEM((d, 2 * f), BF16), pltpu.VMEM((2, f, d), BF16),
                        pltpu.VMEM((tm, f), BF16), pltpu.VMEM((tm, f), BF16),
                        pltpu.VMEM((tm, 2 * f), F32),
                        pltpu.SemaphoreType.DMA((3,))],
    )
    return pl.pallas_call(
        functools.partial(_expert_kernel, layer=layer, f=f),
        grid_spec=grid_spec,
        out_shape=jax.ShapeDtypeStruct((n_rows, dh), xs.dtype),
        compiler_params=_params(1),
        name="experts",
    )(tile_expert, tile_in, tile_out, tile_first, tile_next, tile_slot, tile_slot_prev, n_used,
      xs, w_gate, w_up, w_down)


COMBINE_GROUP = 16


def _combine_ln_kernel(pos_ref, pos_next_ref, wts_ref, ys_ref, h_ref, g_ref, b_ref, ye_hbm, o_ref,
                       buf0, buf1, sem, *, tc, n_tiles):
    i = pl.program_id(0)
    dh = buf0.shape[2]
    d = 2 * dh

    def issue_group(p_ref, buf, s, r0):
        for dn in range(COMBINE_GROUP):
            for k in range(TOP_K):
                pltpu.make_async_copy(ye_hbm.at[pl.ds(p_ref[(r0 + dn) * TOP_K + k], 1)],
                                      buf.at[k, pl.ds(r0 + dn, 1)], sem.at[s]).start()

    def wait_tile(buf, s):
        def body(n, carry):
            for k in range(TOP_K):
                pltpu.make_async_copy(ye_hbm.at[pl.ds(0, 1)], buf.at[k, pl.ds(n, 1)], sem.at[s]).wait()
            return carry
        lax.fori_loop(0, tc, body, 0)

    def compute_group(buf, r0):
        rows = pl.ds(r0, COMBINE_GROUP)
        w = wts_ref[rows, :]
        y_lo = ys_ref[rows, :dh].astype(F32)
        y_hi = ys_ref[rows, dh:].astype(F32)
        for k in range(TOP_K):
            lo, hi = _unpack_bf16_pair(buf[k, rows, :])
            y_lo = y_lo + w[:, k:k + 1] * lo
            y_hi = y_hi + w[:, k:k + 1] * hi
        z_lo = DEEPNORM_ALPHA * h_ref[rows, :dh].astype(F32) + y_lo
        z_hi = DEEPNORM_ALPHA * h_ref[rows, dh:].astype(F32) + y_hi
        mu = (jnp.sum(z_lo, axis=-1, keepdims=True) + jnp.sum(z_hi, axis=-1, keepdims=True)) * (1.0 / d)
        c_lo, c_hi = z_lo - mu, z_hi - mu
        var = (jnp.sum(c_lo * c_lo, axis=-1, keepdims=True)
               + jnp.sum(c_hi * c_hi, axis=-1, keepdims=True)) * (1.0 / d)
        rstd = lax.rsqrt(var + LN_EPS)
        o_ref[rows, :dh] = (c_lo * rstd * g_ref[:, :dh] + b_ref[:, :dh]).astype(o_ref.dtype)
        o_ref[rows, dh:] = (c_hi * rstd * g_ref[:, dh:] + b_ref[:, dh:]).astype(o_ref.dtype)

    n_groups = tc // COMBINE_GROUP

    @pl.when(i == 0)
    def _():
        def first(gi, carry):
            issue_group(pos_ref, buf0, 0, gi * COMBINE_GROUP)
            return carry
        lax.fori_loop(0, n_groups, first, 0)

    def run(cur, cur_s, nxt, nxt_s):
        wait_tile(cur, cur_s)

        def group(gi, carry):
            r0 = pl.multiple_of(gi * COMBINE_GROUP, COMBINE_GROUP)
            issue_group(pos_next_ref, nxt, nxt_s, r0)
            compute_group(cur, r0)
            return carry
        lax.fori_loop(0, n_groups, group, 0)

        @pl.when(i == n_tiles - 1)
        def _():
            wait_tile(nxt, nxt_s)

    @pl.when(i % 2 == 0)
    def _():
        run(buf0, 0, buf1, 1)

    @pl.when(i % 2 == 1)
    def _():
        run(buf1, 1, buf0, 0)


def _combine_ln(pos_flat, wts, ys, h, ye, ln_g, ln_b, *, tc=256):
    t, d = h.shape
    tc = min(tc, t)
    n_tiles = t // tc
    return pl.pallas_call(
        functools.partial(_combine_ln_kernel, tc=tc, n_tiles=n_tiles),
        grid=(n_tiles,),
        in_specs=[pl.BlockSpec((tc * TOP_K,), lambda i: (i,), memory_space=pltpu.SMEM),
                  pl.BlockSpec((tc * TOP_K,), lambda i: (jnp.minimum(i + 1, n_tiles - 1),),
                               memory_space=pltpu.SMEM),
                  pl.BlockSpec((tc, TOP_K), lambda i: (i, 0)),
                  pl.BlockSpec((tc, d), lambda i: (i, 0)),
                  pl.BlockSpec((tc, d), lambda i: (i, 0)),
                  pl.BlockSpec((1, d), lambda i: (0, 0)),
                  pl.BlockSpec((1, d), lambda i: (0, 0)),
                  pl.BlockSpec(memory_space=pl.ANY)],
        out_specs=pl.BlockSpec((tc, d), lambda i: (i, 0)),
        out_shape=jax.ShapeDtypeStruct((t, d), BF16),
        scratch_shapes=[pltpu.VMEM((TOP_K, tc, d // 2), ye.dtype),
                        pltpu.VMEM((TOP_K, tc, d // 2), ye.dtype),
                        pltpu.SemaphoreType.DMA((2,))],
        compiler_params=_params(1),
        name="combine_ln",
    )(pos_flat, pos_flat, wts, ys, h, ln_g.reshape(1, d), ln_b.reshape(1, d), ye)


def _later_matrix(win):
    j = lax.broadcasted_iota(I32, (win, win), 0)
    s = lax.broadcasted_iota(I32, (win, win), 1)
    return jnp.where(j > s, 1.0, 0.0).astype(BF16)


def _attn_kernel(q_ref, k_ref, v_ref, o_ref, *, seq, heads, tq, win0, win):
    inv_sqrt_d = 1.0 / math.sqrt(HEAD_DIM)
    later = {w: _later_matrix(w) for w in {win0, win}}

    def add_window(w, t0, k_start, k_limit, surv, accs):
        ks = pl.multiple_of(k_start, HEAD_DIM)
        lane = lax.broadcasted_iota(I32, (tq, w), 1)
        rowi = lax.broadcasted_iota(I32, (tq, w), 0)
        kpos = k_start + lane
        valid = (kpos < t0 + rowi) & (kpos < k_limit)
        cols = [slice(h * HEAD_DIM, (h + 1) * HEAD_DIM) for h in range(heads)]
        z = [lax.dot_general(q_ref[pl.ds(t0, tq), c], k_ref[pl.ds(ks, w), c],
                             (((1,), (1,)), ((), ())), preferred_element_type=F32) * inv_sqrt_d
             for c in cols]
        softplus = [jnp.maximum(x, 0.0) + jnp.log(1.0 + jnp.exp(-jnp.abs(x))) for x in z]
        log_fail = [jnp.where(valid, -sp, 0.0) for sp in softplus]
        lf_hi = [lf.astype(BF16) for lf in log_fail]
        lf_lo = [(lf - hi.astype(F32)).astype(BF16) for lf, hi in zip(log_fail, lf_hi)]
        between = [_dot(hi, later[w]) + _dot(lo, later[w]) for hi, lo in zip(lf_hi, lf_lo)]
        a = [jnp.where(valid, jnp.exp(x - sp + b + s), 0.0)
             for x, sp, b, s in zip(z, softplus, between, surv)]
        accs = [acc + _dot(p.astype(BF16), v_ref[pl.ds(ks, w), c]) for acc, p, c in zip(accs, a, cols)]
        surv = [s + jnp.sum(lf, axis=-1, keepdims=True) for s, lf in zip(surv, log_fail)]
        return surv, accs

    def any_alive(surv):
        m = surv[0]
        for s in surv[1:]:
            m = jnp.maximum(m, s)
        return (jnp.max(m) >= EXP_ZERO_BELOW).astype(I32)

    def q_tile(qi, carry):
        t0 = pl.multiple_of(qi * tq, tq)
        start0 = jnp.maximum(t0 + tq - win0, 0)
        surv = [jnp.zeros((tq, 1), F32) for _ in range(heads)]
        accs = [jnp.zeros((tq, HEAD_DIM), F32) for _ in range(heads)]
        surv, accs = add_window(win0, t0, start0, seq + win0, surv, accs)

        def cond(state):
            prev_start, alive, _, _ = state
            return (prev_start > 0) & (alive > 0)

        def body(state):
            prev_start, _, surv, accs = state
            start = jnp.maximum(prev_start - win, 0)
            surv, accs = add_window(win, t0, start, prev_start, surv, accs)
            return start, any_alive(surv), surv, accs

        _, _, _, accs = lax.while_loop(cond, body, (start0, any_alive(surv), surv, accs))
        for h in range(heads):
            o_ref[pl.ds(t0, tq), h * HEAD_DIM:(h + 1) * HEAD_DIM] = accs[h].astype(o_ref.dtype)
        return carry

    lax.fori_loop(0, seq // tq, q_tile, 0)


def _attention(q, k, v, batch, seq, *, heads_per_step=4, tq=128, win0=384, win=256):
    t, d = q.shape
    n_heads = d // HEAD_DIM
    hb = min(heads_per_step, n_heads)
    tq, win0, win = min(tq, seq), min(win0, seq), min(win, seq)
    spec = pl.BlockSpec((seq, hb * HEAD_DIM), lambda b, h: (b, h))
    return pl.pallas_call(
        functools.partial(_attn_kernel, seq=seq, heads=hb, tq=tq, win0=win0, win=win),
        grid=(batch, n_heads // hb),
        in_specs=[spec, spec, spec],
        out_specs=spec,
        out_shape=jax.ShapeDtypeStruct((t, d), BF16),
        compiler_params=_params(2),
        name="attention",
    )(q, k, v)


def _moe_sublayer(layer, h_b, h_packed, w_router, bias, w_gate, w_up, w_down, ws_gate, ws_up, ws_down,
                  ln_g, ln_b, *, tile_rows):
    t, d = h_b.shape
    n_exp = w_router.shape[1]
    w_hi = w_router.astype(BF16)
    w_lo = (w_router - w_hi.astype(F32)).astype(BF16)
    idx, wts, rank, cnt = _router(h_b, w_hi, w_lo, bias)

    counts = cnt[0].astype(I32)
    n_tiles_e = (counts + tile_rows - 1) // tile_rows
    tile_end = jnp.cumsum(n_tiles_e)
    tile_start = tile_end - n_tiles_e
    row_off = tile_start * tile_rows
    n_used = tile_end[-1]
    n_tiles = (t * TOP_K) // tile_rows + n_exp
    n_rows = n_tiles * tile_rows

    def expert_of(tile):
        return jnp.sum((tile_end[None, :] <= tile[:, None]).astype(I32), axis=1)

    def lookup(table, e):
        onehot = e[:, None] == jnp.arange(n_exp, dtype=I32)[None, :]
        return jnp.sum(jnp.where(onehot, table[None, :], 0), axis=1)

    step = jnp.arange(n_tiles + 1, dtype=I32)
    tile_in = jnp.minimum(step, n_used - 1)
    tile_out = jnp.minimum(jnp.maximum(step - 1, 0), n_used - 1)
    tile_expert = expert_of(tile_in)
    tile_first = (step == lookup(tile_start, tile_expert)).astype(I32)
    nxt_tile = lookup(tile_end, tile_expert)
    tile_next = jnp.where(nxt_tile < n_used, expert_of(jnp.minimum(nxt_tile, n_used - 1)), -1).astype(I32)
    tile_slot = (jnp.cumsum(tile_first) - 1) % 2
    tile_slot_prev = jnp.concatenate([tile_slot[:1], tile_slot[:-1]])

    pos = _positions(idx, rank, row_off.astype(F32).reshape(1, n_exp))
    pos_flat = pos.reshape(t * TOP_K)
    xs, ys = _dispatch_shared(pos_flat, h_packed, h_b, ws_gate.astype(BF16), ws_up.astype(BF16),
                              ws_down.astype(BF16), (row_off + counts).astype(I32),
                              (row_off + n_tiles_e * tile_rows).astype(I32), n_rows)
    ye = _experts(tile_expert, tile_in, tile_out, tile_first, tile_next, tile_slot.astype(I32),
                  tile_slot_prev.astype(I32), n_used.reshape(1).astype(I32),
                  xs, w_gate, w_up, w_down, layer=layer, tm=tile_rows)
    return _combine_ln(pos_flat, wts, ys, h_b, ye, ln_g, ln_b)


def kernel(x, p, pool_w_in, pool_w_grp, pool_scale, pool_w_out, kv_w_k, kv_w_v, sb_w_q, sb_w_o,
           moe_w_router, moe_bias, moe_w_gate, moe_w_up, moe_w_down, shared_w_gate, shared_w_up,
           shared_w_down, ple_w_proj, ple_w_gate, ple_b_gate, ln_g, ln_b):
    batch, seq, d = x.shape
    t = batch * seq
    xf = x.reshape(t, d)
    pb = p.reshape(p.shape[0], t, p.shape[-1]).astype(BF16)
    bf = lambda w: w.astype(BF16)

    def moe_and_ple(i, h_b, h_packed, last):
        h_b = _moe_sublayer(i, h_b, h_packed, moe_w_router[i], moe_bias[i], moe_w_gate, moe_w_up,
                            moe_w_down, shared_w_gate[i], shared_w_up[i], shared_w_down[i],
                            ln_g[i, 1], ln_b[i, 1], tile_rows=EXPERT_TILE_ROWS)
        return _mm_ln(h_b, bf(ple_w_gate[i]), h_b, ln_g[i, 2], ln_b[i, 2],
                      out_kinds=(F32,) if last else (BF16,),
                      gate=(pb[i], bf(ple_w_proj[i]), ple_b_gate[i]),
                      tn=512 if last else 1024)[0]

    u = _matmul(bf(xf), pool_w_in[0])
    mixed = _pool_grp(u, bf(pool_w_grp[0]), pool_scale[0], seq)
    h_b, h_packed = _mm_ln(mixed, bf(pool_w_out[0]), xf, ln_g[0, 0], ln_b[0, 0], out_kinds=(BF16, PACKED))
    h_b = moe_and_ple(0, h_b, h_packed, last=False)

    kk = _matmul(h_b, kv_w_k)
    vv = _matmul(h_b, kv_w_v)
    qq = _matmul(h_b, sb_w_q[0])
    o = _attention(qq, kk, vv, batch, seq)
    h_b, h_packed = _mm_ln(o, bf(sb_w_o[0]), h_b, ln_g[1, 0], ln_b[1, 0], out_kinds=(BF16, PACKED),
                           tn=1024)
    out = moe_and_ple(1, h_b, h_packed, last=True)
    return out.reshape(batch, seq, d)
```
